```python
import jax, jax.numpy as jnp
from jax import lax
import numpy as np

D_MODEL = 2048
BATCH = 2
SEQ = 16384
DEPTH = 1

D_MIX = D_MODEL
HEAD_DIM = 64
N_Q_HEADS = 16
N_KV_HEADS = 4
Q_PER_KV = N_Q_HEADS // N_KV_HEADS
ATTN_WIDTH = N_Q_HEADS * HEAD_DIM
KV_WIDTH = N_KV_HEADS * HEAD_DIM
WINDOW = 128
BLOCK = 128
ROT_DIM = HEAD_DIM // 4
ROPE_THETA = 500000.0
CHUNK = 128
GMLP_GROUP_DIM = 128
GMLP_WIDTH = D_MIX - ATTN_WIDTH
N_GMLP_GROUPS = GMLP_WIDTH // GMLP_GROUP_DIM
IN_COLS = ATTN_WIDTH + 2 * KV_WIDTH + 2 * GMLP_WIDTH
D_FF = 5632
EPS = 1e-6
NEG_INF = -1e30

kernel_name = "hymba_swa_sink_gmlp_macaron"


def rmsnorm(x, g):
    xf = x.astype(jnp.float32)
    y = xf * lax.rsqrt(jnp.mean(xf * xf, axis=-1, keepdims=True) + EPS)
    return (y * g.astype(jnp.float32)).astype(x.dtype)


def swiglu(h, w_gate, w_up, w_down):
    return (jax.nn.silu(h @ w_gate) * (h @ w_up)) @ w_down


def rope_tables(positions, dtype):
    inv_freq = ROPE_THETA ** (-jnp.arange(0, ROT_DIM, 2, dtype=jnp.float32) / ROT_DIM)
    ang = positions.astype(jnp.float32)[..., None] * inv_freq
    return jnp.cos(ang)[:, :, None, :].astype(dtype), jnp.sin(ang)[:, :, None, :].astype(dtype)


def partial_rope(t, cos, sin):
    half = ROT_DIM // 2
    t1, t2, rest = t[..., :half], t[..., half:ROT_DIM], t[..., ROT_DIM:]
    return jnp.concatenate([t1 * cos - t2 * sin, t2 * cos + t1 * sin, rest], axis=-1)


def band(t):
    b, s, h, d = t.shape
    tb = t.reshape(b, s // BLOCK, BLOCK, h, d)
    prev = jnp.pad(tb, ((0, 0), (1, 0), (0, 0), (0, 0), (0, 0)))[:, :-1]
    return jnp.concatenate([prev, tb], axis=2)


def sliding_window_attention_with_sinks(q, k, v, sinks):
    b, s = q.shape[:2]
    nb = s // BLOCK
    qb = q.reshape(b, nb, BLOCK, N_KV_HEADS, Q_PER_KV, HEAD_DIM)
    kb, vb = band(k), band(v)
    scores = jnp.einsum('bnqhgd,bnkhd->bnhgqk', qb, kb,
                        preferred_element_type=jnp.float32) * (HEAD_DIM ** -0.5)
    qi = jnp.arange(BLOCK)[:, None]
    kj = jnp.arange(2 * BLOCK)[None, :]
    diff = qi + BLOCK - kj
    blk = jnp.arange(nb)[:, None, None]
    valid = (diff >= 0) & (diff < WINDOW) & (blk * BLOCK + kj - BLOCK >= 0)
    scores = jnp.where(valid[None, :, None, None], scores, NEG_INF)
    sink = sinks.astype(jnp.float32).reshape(N_KV_HEADS, Q_PER_KV)[None, None, :, :, None, None]
    m = jnp.maximum(jnp.max(scores, axis=-1, keepdims=True), sink)
    p = jnp.exp(scores - m)
    p = p / (jnp.sum(p, axis=-1, keepdims=True) + jnp.exp(sink - m))
    out = jnp.einsum('bnhgqk,bnkhd->bnqhgd', p.astype(v.dtype), vb)
    return out.reshape(b, s, ATTN_WIDTH)


def chunked_spatial_gating(u, v, w_s, b_s):
    b, s = u.shape[:2]
    nc = s // CHUNK
    v4 = v.reshape(b, nc, CHUNK, N_GMLP_GROUPS, GMLP_GROUP_DIM)
    causal = jnp.tril(jnp.ones((CHUNK, CHUNK), dtype=bool))
    w = jnp.where(causal[None], w_s, jnp.zeros_like(w_s))
    sp = jnp.einsum('gts,bnsgc->bntgc', w, v4) + b_s.T[None, None, :, :, None]
    return u * sp.reshape(b, s, GMLP_WIDTH)


def setup_inputs(seed: int = 0) -> dict:
    key = jax.random.key(seed)
    ks = jax.random.split(key, 24)
    f32 = jnp.float32

    def nrm(k, shape, scale):
        return jax.random.normal(k, shape, f32) * scale

    def gain(k, shape):
        return 1.0 + 0.1 * jax.random.normal(k, shape, f32)

    L = DEPTH
    x = jax.random.normal(ks[0], (BATCH, SEQ, D_MODEL), f32)
    positions = (jnp.arange(SEQ, dtype=jnp.int32)[None, :]
                 + jax.random.randint(ks[1], (BATCH, 1), 0, 4096, dtype=jnp.int32))
    return {
        "x": x,
        "positions": positions,
        "ffn1_norm": gain(ks[2], (L, D_MODEL)),
        "ffn1_w_gate": nrm(ks[3], (L, D_MODEL, D_FF), D_MODEL ** -0.5),
        "ffn1_w_up": nrm(ks[4], (L, D_MODEL, D_FF), D_MODEL ** -0.5),
        "ffn1_w_down": nrm(ks[5], (L, D_FF, D_MODEL), D_FF ** -0.5),
        "mix_norm": gain(ks[6], (L, D_MODEL)),
        "w_in": nrm(ks[7], (L, D_MODEL, IN_COLS), D_MODEL ** -0.5),
        "q_norm": gain(ks[8], (L, HEAD_DIM)),
        "k_norm": gain(ks[9], (L, HEAD_DIM)),
        "attn_sinks": nrm(ks[10], (L, N_Q_HEADS), 1.0),
        "gmlp_v_norm": gain(ks[11], (L, GMLP_WIDTH)),
        "gmlp_w_s": nrm(ks[12], (L, N_GMLP_GROUPS, CHUNK, CHUNK), CHUNK ** -0.5),
        "gmlp_b_s": 1.0 + 0.1 * jax.random.normal(ks[13], (L, N_GMLP_GROUPS, CHUNK), f32),
        "attn_out_norm": gain(ks[14], (L, ATTN_WIDTH)),
        "gmlp_out_norm": gain(ks[15], (L, GMLP_WIDTH)),
        "w_out": nrm(ks[16], (L, D_MIX, D_MODEL), D_MIX ** -0.5),
        "ffn2_norm": gain(ks[17], (L, D_MODEL)),
        "ffn2_w_gate": nrm(ks[18], (L, D_MODEL, D_FF), D_MODEL ** -0.5),
        "ffn2_w_up": nrm(ks[19], (L, D_MODEL, D_FF), D_MODEL ** -0.5),
        "ffn2_w_down": nrm(ks[20], (L, D_FF, D_MODEL), D_FF ** -0.5),
    }


def reference(x, positions, ffn1_norm, ffn1_w_gate, ffn1_w_up, ffn1_w_down, mix_norm, w_in,
              q_norm, k_norm, attn_sinks, gmlp_v_norm, gmlp_w_s, gmlp_b_s, attn_out_norm,
              gmlp_out_norm, w_out, ffn2_norm, ffn2_w_gate, ffn2_w_up, ffn2_w_down):
    b, s, _ = x.shape
    cos, sin = rope_tables(positions, x.dtype)
    splits = np.cumsum([ATTN_WIDTH, KV_WIDTH, KV_WIDTH, GMLP_WIDTH]).tolist()
    for l in range(DEPTH):
        x = x + 0.5 * swiglu(rmsnorm(x, ffn1_norm[l]), ffn1_w_gate[l], ffn1_w_up[l], ffn1_w_down[l])

        h = rmsnorm(x, mix_norm[l])
        z = h @ w_in[l]
        q, k, v, gu, gv = jnp.split(z, splits, axis=-1)

        q = rmsnorm(q.reshape(b, s, N_Q_HEADS, HEAD_DIM), q_norm[l])
        k = rmsnorm(k.reshape(b, s, N_KV_HEADS, HEAD_DIM), k_norm[l])
        v = v.reshape(b, s, N_KV_HEADS, HEAD_DIM)
        q = partial_rope(q, cos, sin)
        k = partial_rope(k, cos, sin)
        a_out = sliding_window_attention_with_sinks(q, k, v, attn_sinks[l])

        gu = jax.nn.gelu(gu)
        gv = jax.nn.gelu(gv).reshape(b, s, N_GMLP_GROUPS, GMLP_GROUP_DIM)
        gv = rmsnorm(gv, gmlp_v_norm[l].reshape(N_GMLP_GROUPS, GMLP_GROUP_DIM)).reshape(b, s, GMLP_WIDTH)
        g_out = chunked_spatial_gating(gu, gv, gmlp_w_s[l], gmlp_b_s[l])

        mixed = jnp.concatenate([rmsnorm(a_out, attn_out_norm[l]),
                                 rmsnorm(g_out, gmlp_out_norm[l])], axis=-1)
        x = x + mixed @ w_out[l]

        x = x + 0.5 * swiglu(rmsnorm(x, ffn2_norm[l]), ffn2_w_gate[l], ffn2_w_up[l], ffn2_w_down[l])
    return x
```

```python
import functools

import jax
import jax.numpy as jnp
from jax import lax
from jax.experimental import pallas as pl
from jax.experimental.pallas import tpu as pltpu

HEAD_DIM = 64
N_Q_HEADS = 16
N_KV_HEADS = 4
ATTN_WIDTH = N_Q_HEADS * HEAD_DIM
KV_WIDTH = N_KV_HEADS * HEAD_DIM
BLOCK = 128
ROT_DIM = HEAD_DIM // 4
ROPE_THETA = 500000.0
GMLP_GROUP_DIM = 128
GMLP_WIDTH = 1024
N_GMLP_GROUPS = GMLP_WIDTH // GMLP_GROUP_DIM
EPS = 1e-6
NEG_INF = -1e30

LANES = 128
V7X_VMEM_LIMIT_BYTES = 60000 * 1024

FFN_TOKEN_TILE = 512
FFN_FF_TILE = 512
MIXER_TOKEN_TILE = 256

F32 = jnp.float32
BF16 = jnp.bfloat16


def _rmsnorm_rows(x, gain):
    y = x * lax.rsqrt(jnp.mean(x * x, axis=-1, keepdims=True) + EPS)
    return y * gain


def _ffn_kernel(x_ref, gain_ref, wg_ref, wu_ref, wd_ref, o_ref, h_ref, *, n_ff_tiles):
    f = pl.program_id(1)

    @pl.when(f == 0)
    def _():
        h_ref[...] = _rmsnorm_rows(x_ref[...], gain_ref[...]).astype(BF16)

    h = h_ref[...]
    g = jnp.dot(h, wg_ref[...], preferred_element_type=F32)
    u = jnp.dot(h, wu_ref[...], preferred_element_type=F32)
    a = (jax.nn.silu(g) * u).astype(BF16)
    d = jnp.dot(a, wd_ref[...], preferred_element_type=F32)

    @pl.when(f == 0)
    def _():
        o_ref[...] = d

    @pl.when(f > 0)
    def _():
        o_ref[...] += d

    @pl.when(f == n_ff_tiles - 1)
    def _():
        o_ref[...] = x_ref[...] + 0.5 * o_ref[...]


def _ffn(x, gain, w_gate, w_up, w_down):
    t, d = x.shape
    d_ff = w_gate.shape[1]
    tm, tf = FFN_TOKEN_TILE, FFN_FF_TILE
    assert t % tm == 0 and d_ff % tf == 0
    n_ff_tiles = d_ff // tf
    return pl.pallas_call(
        functools.partial(_ffn_kernel, n_ff_tiles=n_ff_tiles),
        out_shape=jax.ShapeDtypeStruct((t, d), F32),
        grid=(t // tm, n_ff_tiles),
        in_specs=[
            pl.BlockSpec((tm, d), lambda i, f: (i, 0)),
            pl.BlockSpec((1, d), lambda i, f: (0, 0)),
            pl.BlockSpec((d, tf), lambda i, f: (0, f)),
            pl.BlockSpec((d, tf), lambda i, f: (0, f)),
            pl.BlockSpec((tf, d), lambda i, f: (f, 0)),
        ],
        out_specs=pl.BlockSpec((tm, d), lambda i, f: (i, 0)),
        scratch_shapes=[pltpu.VMEM((tm, d), BF16)],
        compiler_params=pltpu.CompilerParams(
            dimension_semantics=("arbitrary", "arbitrary"),
            vmem_limit_bytes=V7X_VMEM_LIMIT_BYTES,
        ),
        name="ffn",
    )(x, gain, w_gate, w_up, w_down)


def _block_diag_ones(size, group):
    r = lax.broadcasted_iota(jnp.int32, (size, size), 0) // group
    c = lax.broadcasted_iota(jnp.int32, (size, size), 1) // group
    return jnp.where(r == c, 1.0, 0.0).astype(BF16)


def _group_sumsq(z, bd):
    return jnp.dot((z * z).astype(BF16), bd, preferred_element_type=F32)


def _mixer_kernel(x_ref, pos_ref, invf_ref, gm_ref, win_ref, qg_ref, kg_ref, sinks_ref,
                  gvg_ref, ws_ref, bst_ref, ag_ref, gg_ref, wout_ref, o_ref,
                  kprev_ref, vprev_ref, a_ref, g_ref, *, tb, tiles_per_seq):
    i = pl.program_id(0)
    n_blk = tb // BLOCK

    @pl.when(i == 0)
    def _():
        kprev_ref[...] = jnp.zeros_like(kprev_ref)
        vprev_ref[...] = jnp.zeros_like(vprev_ref)

    x = x_ref[...]
    h = _rmsnorm_rows(x, gm_ref[...]).astype(BF16)

    lane = lax.broadcasted_iota(jnp.int32, (1, LANES), 1)
    r = lane % HEAD_DIM
    lo_half = lane < HEAD_DIM

    ang = pos_ref[...].astype(F32) * invf_ref[...]
    cos, sin = jnp.cos(ang), jnp.sin(ang)
    half = ROT_DIM // 2
    rope_c = jnp.where(r < ROT_DIM, cos, 1.0)
    rope_s1 = jnp.where(r < half, -sin, 0.0)
    rope_s2 = jnp.where((r >= half) & (r < ROT_DIM), sin, 0.0)

    bd_head = _block_diag_ones(2 * LANES, HEAD_DIM)
    bd_group = _block_diag_ones(2 * LANES, GMLP_GROUP_DIM)

    def head_norm_rope(z, gain):
        zn = z * lax.rsqrt(_group_sumsq(z, bd_head) * (1.0 / HEAD_DIM) + EPS)
        slabs = []
        for j in range(2):
            t = zn[:, j * LANES:(j + 1) * LANES] * gain
            t = (t * rope_c + pltpu.roll(t, LANES - half, 1) * rope_s1
                 + pltpu.roll(t, half, 1) * rope_s2)
            slabs.append(t)
        return slabs

    q_slabs = []
    for c in range(ATTN_WIDTH // (2 * LANES)):
        zq = jnp.dot(h, win_ref[:, c * 2 * LANES:(c + 1) * 2 * LANES], preferred_element_type=F32)
        q_slabs += [s.astype(BF16) for s in head_norm_rope(zq, qg_ref[...])]
    col = ATTN_WIDTH
    zk = jnp.dot(h, win_ref[:, col:col + KV_WIDTH], preferred_element_type=F32)
    k_slabs = head_norm_rope(zk, kg_ref[...])
    col += KV_WIDTH
    zv = jnp.dot(h, win_ref[:, col:col + KV_WIDTH], preferred_element_type=F32)
    v_slabs = [zv[:, :LANES], zv[:, LANES:]]
    col += KV_WIDTH

    def padded_pair(cur, prev_ref, j):
        full = jnp.concatenate([prev_ref[:, j * LANES:(j + 1) * LANES], cur], axis=0)
        rolled = pltpu.roll(full, HEAD_DIM, 1)
        zero = jnp.zeros_like(full)
        even = (jnp.where(lo_half, full, zero), jnp.where(lo_half, zero, rolled))
        odd = (jnp.where(lo_half, rolled, zero), jnp.where(lo_half, zero, full))
        return [tuple(t.astype(BF16) for t in even), tuple(t.astype(BF16) for t in odd)]

    k_pad, v_pad = [], []
    for j in range(2):
        k_pad += padded_pair(k_slabs[j], kprev_ref, j)
        v_pad += padded_pair(v_slabs[j], vprev_ref, j)
    ones_lo = jnp.broadcast_to(jnp.where(lo_half, 1.0, 0.0).astype(BF16), (2 * BLOCK, LANES))
    ones_hi = jnp.broadcast_to(jnp.where(lo_half, 0.0, 1.0).astype(BF16), (2 * BLOCK, LANES))

    kprev_ref[...] = jnp.concatenate([s[tb - BLOCK:, :] for s in k_slabs], axis=1)
    vprev_ref[...] = jnp.concatenate([s[tb - BLOCK:, :] for s in v_slabs], axis=1)

    srow = lax.broadcasted_iota(jnp.int32, (2 * BLOCK, 4 * BLOCK), 0) % BLOCK
    scol = lax.broadcasted_iota(jnp.int32, (2 * BLOCK, 4 * BLOCK), 1) % (2 * BLOCK)
    row_first_slab = lax.broadcasted_iota(jnp.int32, (2 * BLOCK, 1), 0) < BLOCK
    seq_start = (i % tiles_per_seq) == 0
    for b in range(n_blk):
        prev_off = jnp.where(seq_start, BLOCK, 0) if b == 0 else 0
        valid = (((scol < BLOCK) & (scol > srow + prev_off))
                 | ((scol >= BLOCK) & (scol - BLOCK <= srow)))
        rows = slice(b * BLOCK, (b + 2) * BLOCK)
        for hkv in range(N_KV_HEADS):
            q2 = jnp.concatenate([q_slabs[2 * hkv][b * BLOCK:(b + 1) * BLOCK],
                                  q_slabs[2 * hkv + 1][b * BLOCK:(b + 1) * BLOCK]], axis=0)
            kcat = jnp.concatenate([k_pad[hkv][0][rows], k_pad[hkv][1][rows]], axis=0)
            s = lax.dot_general(q2, kcat, (((1,), (1,)), ((), ())), preferred_element_type=F32)
            s = jnp.where(valid, s, NEG_INF)
            sink_lo = jnp.where(row_first_slab, sinks_ref[4 * hkv], sinks_ref[4 * hkv + 2])
            sink_hi = jnp.where(row_first_slab, sinks_ref[4 * hkv + 1], sinks_ref[4 * hkv + 3])
            s_lo, s_hi = s[:, :2 * BLOCK], s[:, 2 * BLOCK:]
            m_lo = jnp.maximum(jnp.max(s_lo, axis=1, keepdims=True), sink_lo)
            m_hi = jnp.maximum(jnp.max(s_hi, axis=1, keepdims=True), sink_hi)
            p = jnp.concatenate([jnp.exp(s_lo - m_lo), jnp.exp(s_hi - m_hi)], axis=1).astype(BF16)
            vcat = jnp.concatenate(
                [jnp.concatenate([v_pad[hkv][0][rows], ones_lo], axis=1),
                 jnp.concatenate([v_pad[hkv][1][rows], ones_hi], axis=1)], axis=0)
            o2 = jnp.dot(p, vcat, preferred_element_type=F32)
            den = o2[:, LANES:] + jnp.where(lo_half, jnp.exp(sink_lo - m_lo), jnp.exp(sink_hi - m_hi))
            out = o2[:, :LANES] / den
            a_ref[b * BLOCK:(b + 1) * BLOCK, (2 * hkv) * LANES:(2 * hkv + 1) * LANES] = out[:BLOCK]
            a_ref[b * BLOCK:(b + 1) * BLOCK, (2 * hkv + 1) * LANES:(2 * hkv + 2) * LANES] = out[BLOCK:]

    trow = lax.broadcasted_iota(jnp.int32, (BLOCK, BLOCK), 0)
    tcol = lax.broadcasted_iota(jnp.int32, (BLOCK, BLOCK), 1)
    causal = tcol <= trow
    for c in range(GMLP_WIDTH // (2 * LANES)):
        cu = col + c * 2 * LANES
        cv = col + GMLP_WIDTH + c * 2 * LANES
        gu = jax.nn.gelu(jnp.dot(h, win_ref[:, cu:cu + 2 * LANES], preferred_element_type=F32))
        gv = jax.nn.gelu(jnp.dot(h, win_ref[:, cv:cv + 2 * LANES], preferred_element_type=F32))
        gv = gv * lax.rsqrt(_group_sumsq(gv, bd_group) * (1.0 / GMLP_GROUP_DIM) + EPS)
        gv = (gv * gvg_ref[:, c * 2 * LANES:(c + 1) * 2 * LANES]).astype(BF16)
        for jj in range(2):
            grp = 2 * c + jj
            w = jnp.where(causal, ws_ref[grp], 0.0).astype(BF16)
            bias = bst_ref[:, grp:grp + 1]
            for b in range(n_blk):
                rws = slice(b * BLOCK, (b + 1) * BLOCK)
                sp = jnp.dot(w, gv[rws, jj * LANES:(jj + 1) * LANES], preferred_element_type=F32) + bias
                g_ref[rws, grp * LANES:(grp + 1) * LANES] = gu[rws, jj * LANES:(jj + 1) * LANES] * sp

    mixed = jnp.concatenate([_rmsnorm_rows(a_ref[...], ag_ref[...]),
                             _rmsnorm_rows(g_ref[...], gg_ref[...])], axis=1).astype(BF16)
    o_ref[...] = x + jnp.dot(mixed, wout_ref[...], preferred_element_type=F32)


def _mixer(x, pos, invf, gm, w_in, qg, kg, sinks, gvg, w_s, bst, ag, gg, w_out, *, seq_len):
    t, d = x.shape
    tb = MIXER_TOKEN_TILE
    assert t % tb == 0 and seq_len % tb == 0 and tb % BLOCK == 0

    def resident(shape):
        return pl.BlockSpec(shape, lambda i: (0,) * len(shape), pipeline_mode=pl.Buffered(1))

    return pl.pallas_call(
        functools.partial(_mixer_kernel, tb=tb, tiles_per_seq=seq_len // tb),
        out_shape=jax.ShapeDtypeStruct((t, d), F32),
        grid=(t // tb,),
        in_specs=[
            pl.BlockSpec((tb, d), lambda i: (i, 0)),
            pl.BlockSpec((tb, 1), lambda i: (i, 0)),
            resident(invf.shape),
            resident(gm.shape),
            resident(w_in.shape),
            resident(qg.shape),
            resident(kg.shape),
            pl.BlockSpec(memory_space=pltpu.SMEM),
            resident(gvg.shape),
            resident(w_s.shape),
            resident(bst.shape),
            resident(ag.shape),
            resident(gg.shape),
            resident(w_out.shape),
        ],
        out_specs=pl.BlockSpec((tb, d), lambda i: (i, 0)),
        scratch_shapes=[
            pltpu.VMEM((BLOCK, KV_WIDTH), F32),
            pltpu.VMEM((BLOCK, KV_WIDTH), F32),
            pltpu.VMEM((tb, ATTN_WIDTH), F32),
            pltpu.VMEM((tb, GMLP_WIDTH), F32),
        ],
        compiler_params=pltpu.CompilerParams(
            dimension_semantics=("arbitrary",),
            vmem_limit_bytes=V7X_VMEM_LIMIT_BYTES,
        ),
        name="mixer",
    )(x, pos, invf, gm, w_in, qg, kg, sinks, gvg, w_s, bst, ag, gg, w_out)


def kernel(x, positions, ffn1_norm, ffn1_w_gate, ffn1_w_up, ffn1_w_down, mix_norm, w_in,
           q_norm, k_norm, attn_sinks, gmlp_v_norm, gmlp_w_s, gmlp_b_s, attn_out_norm,
           gmlp_out_norm, w_out, ffn2_norm, ffn2_w_gate, ffn2_w_up, ffn2_w_down):
    b, s, d = x.shape
    depth = w_in.shape[0]
    xt = x.reshape(b * s, d)
    pos = positions.reshape(b * s, 1)

    inv_freq = ROPE_THETA ** (-jnp.arange(0, ROT_DIM, 2, dtype=F32) / ROT_DIM)
    lane = jnp.arange(LANES)
    invf = inv_freq[(lane % HEAD_DIM) % (ROT_DIM // 2)].reshape(1, LANES)

    for l in range(depth):
        xt = _ffn(xt, ffn1_norm[l].reshape(1, d), ffn1_w_gate[l].astype(BF16),
                  ffn1_w_up[l].astype(BF16), ffn1_w_down[l].astype(BF16))
        xt = _mixer(
            xt, pos, invf, mix_norm[l].reshape(1, d), w_in[l].astype(BF16),
            (jnp.tile(q_norm[l], 2) * (HEAD_DIM ** -0.5)).reshape(1, LANES),
            jnp.tile(k_norm[l], 2).reshape(1, LANES),
            attn_sinks[l], gmlp_v_norm[l].reshape(1, GMLP_WIDTH), gmlp_w_s[l], gmlp_b_s[l].T,
            attn_out_norm[l].reshape(1, ATTN_WIDTH), gmlp_out_norm[l].reshape(1, GMLP_WIDTH),
            w_out[l].astype(BF16), seq_len=s)
        xt = _ffn(xt, ffn2_norm[l].reshape(1, d), ffn2_w_gate[l].astype(BF16),
                  ffn2_w_up[l].astype(BF16), ffn2_w_down[l].astype(BF16))
    return xt.reshape(b, s, d)
```

```python
import functools

import jax
import jax.numpy as jnp
from jax import lax
from jax.experimental import pallas as pl
from jax.experimental.pallas import tpu as pltpu

HEAD_DIM = 64
N_Q_HEADS = 16
N_KV_HEADS = 4
ATTN_WIDTH = N_Q_HEADS * HEAD_DIM
KV_WIDTH = N_KV_HEADS * HEAD_DIM
BLOCK = 128
ROT_DIM = HEAD_DIM // 4
ROPE_THETA = 500000.0
GMLP_GROUP_DIM = 128
GMLP_WIDTH = 1024
N_GMLP_GROUPS = GMLP_WIDTH // GMLP_GROUP_DIM
EPS = 1e-6
NEG_INF = -1e30

LANES = 128
V7X_VMEM_LIMIT_BYTES = 60000 * 1024

FFN_TOKEN_TILE = 1024
FFN_FF_TILE = 512
MIXER_TOKEN_TILE = 256

F32 = jnp.float32
BF16 = jnp.bfloat16


def _rmsnorm_rows(x, gain):
    y = x * lax.rsqrt(jnp.mean(x * x, axis=-1, keepdims=True) + EPS)
    return y * gain


def _ffn_kernel(x_ref, gain_ref, wg_ref, wu_ref, wd_ref, o_ref, h_ref):
    @pl.when(pl.program_id(1) == 0)
    def _():
        x = x_ref[...]
        h_ref[...] = _rmsnorm_rows(x, gain_ref[...]).astype(BF16)
        o_ref[...] = x

    h = h_ref[...]
    g = jnp.dot(h, wg_ref[...], preferred_element_type=F32)
    u = jnp.dot(h, wu_ref[...], preferred_element_type=F32)
    a = (jax.nn.silu(g) * (0.5 * u)).astype(BF16)
    o_ref[...] += jnp.dot(a, wd_ref[...], preferred_element_type=F32)


def _ffn(x, gain, w_gate, w_up, w_down):
    t, d = x.shape
    d_ff = w_gate.shape[1]
    tm, tf = FFN_TOKEN_TILE, FFN_FF_TILE
    assert t % tm == 0 and d_ff % tf == 0
    return pl.pallas_call(
        _ffn_kernel,
        out_shape=jax.ShapeDtypeStruct((t, d), F32),
        grid=(t // tm, d_ff // tf),
        in_specs=[
            pl.BlockSpec((tm, d), lambda i, f: (i, 0)),
            pl.BlockSpec((1, d), lambda i, f: (0, 0)),
            pl.BlockSpec((d, tf), lambda i, f: (0, f)),
            pl.BlockSpec((d, tf), lambda i, f: (0, f)),
            pl.BlockSpec((tf, d), lambda i, f: (f, 0)),
        ],
        out_specs=pl.BlockSpec((tm, d), lambda i, f: (i, 0)),
        scratch_shapes=[pltpu.VMEM((tm, d), BF16)],
        compiler_params=pltpu.CompilerParams(
            dimension_semantics=("arbitrary", "arbitrary"),
            vmem_limit_bytes=V7X_VMEM_LIMIT_BYTES,
        ),
        name="ffn",
    )(x, gain, w_gate, w_up, w_down)


def _block_diag_ones(size, group):
    r = lax.broadcasted_iota(jnp.int32, (size, size), 0) // group
    c = lax.broadcasted_iota(jnp.int32, (size, size), 1) // group
    return jnp.where(r == c, 1.0, 0.0).astype(BF16)


def _group_sumsq(z, bd):
    return jnp.dot((z * z).astype(BF16), bd, preferred_element_type=F32)


def _mixer_kernel(x_ref, pos_ref, invf_ref, gm_ref, win_ref, qg_ref, kg_ref, sinks_ref,
                  gvg_ref, ws_ref, bst_ref, ag_ref, gg_ref, wout_ref, o_ref,
                  h_ref, z_ref, q_ref, kpad_ref, vpad_ref, a_ref, g_ref, m_ref,
                  *, tb, tiles_per_seq):
    i = pl.program_id(0)
    n_blk = tb // BLOCK
    chunk = 2 * LANES
    half = ROT_DIM // 2
    col_k = ATTN_WIDTH
    col_v = col_k + KV_WIDTH
    col_gu = col_v + KV_WIDTH
    col_gv = col_gu + GMLP_WIDTH

    @pl.when(i == 0)
    def _():
        kpad_ref[:, :BLOCK, :] = jnp.zeros((2 * N_KV_HEADS, BLOCK, LANES), BF16)
        vpad_ref[:, :BLOCK, :] = jnp.zeros((2 * N_KV_HEADS, BLOCK, LANES), BF16)

    lane = lax.broadcasted_iota(jnp.int32, (1, LANES), 1)
    lo_half = lane < HEAD_DIM

    ang = invf_ref[...] * pos_ref[...].astype(F32)
    cos, sin = jnp.cos(ang), jnp.sin(ang)
    rest_one = jnp.ones((HEAD_DIM - ROT_DIM, tb), F32)
    rest_zero = jnp.zeros((HEAD_DIM - ROT_DIM, tb), F32)
    zero = jnp.zeros((half, tb), F32)
    rope_c = jnp.concatenate([cos, cos, rest_one] * 2, axis=0).T
    rope_s1 = jnp.concatenate([-sin, zero, rest_zero] * 2, axis=0).T
    rope_s2 = jnp.concatenate([zero, sin, rest_zero] * 2, axis=0).T

    bd_head = _block_diag_ones(chunk, HEAD_DIM)
    bd_group = _block_diag_ones(chunk, GMLP_GROUP_DIM)

    def head_norm_rope(z, gain):
        zn = z * lax.rsqrt(_group_sumsq(z, bd_head) * (1.0 / HEAD_DIM) + EPS)
        slabs = []
        for j in range(2):
            t = zn[:, j * LANES:(j + 1) * LANES] * gain
            slabs.append(t * rope_c + pltpu.roll(t, LANES - half, 1) * rope_s1
                         + pltpu.roll(t, half, 1) * rope_s2)
        return slabs

    def x_norm():
        h_ref[...] = _rmsnorm_rows(x_ref[...], gm_ref[...]).astype(BF16)

    def proj(c0):
        cols = slice(c0, c0 + chunk)
        z_ref[:, cols] = jnp.dot(h_ref[...], win_ref[:, cols], preferred_element_type=F32)

    def q_norm(c):
        cols = slice(c * chunk, (c + 1) * chunk)
        s0, s1 = head_norm_rope(z_ref[:, cols], qg_ref[...])
        q_ref[:, cols] = jnp.concatenate([s0, s1], axis=1).astype(BF16)

    def place_heads(slab, dst_ref, j):
        rolled = pltpu.roll(slab, HEAD_DIM, 1)
        zero_slab = jnp.zeros_like(slab)
        cur = slice(BLOCK, BLOCK + tb)
        dst_ref[4 * j + 0, cur, :] = jnp.where(lo_half, slab, zero_slab).astype(BF16)
        dst_ref[4 * j + 1, cur, :] = jnp.where(lo_half, zero_slab, rolled).astype(BF16)
        dst_ref[4 * j + 2, cur, :] = jnp.where(lo_half, rolled, zero_slab).astype(BF16)
        dst_ref[4 * j + 3, cur, :] = jnp.where(lo_half, zero_slab, slab).astype(BF16)

    def kv_place():
        k_slabs = head_norm_rope(z_ref[:, col_k:col_k + KV_WIDTH], kg_ref[...])
        for j in range(2):
            place_heads(k_slabs[j], kpad_ref, j)
            place_heads(z_ref[:, col_v + j * LANES:col_v + (j + 1) * LANES], vpad_ref, j)

    srow = lax.broadcasted_iota(jnp.int32, (2 * BLOCK, 4 * BLOCK), 0) % BLOCK
    scol = lax.broadcasted_iota(jnp.int32, (2 * BLOCK, 4 * BLOCK), 1) % (2 * BLOCK)
    row_first_slab = lax.broadcasted_iota(jnp.int32, (2 * BLOCK, 1), 0) < BLOCK
    seq_start = (i % tiles_per_seq) == 0
    ones_lo = jnp.broadcast_to(jnp.where(lo_half, 1.0, 0.0).astype(BF16), (2 * BLOCK, LANES))
    ones_hi = jnp.broadcast_to(jnp.where(lo_half, 0.0, 1.0).astype(BF16), (2 * BLOCK, LANES))

    def band_mask(b):
        prev_off = jnp.where(seq_start, BLOCK, 0) if b == 0 else 0
        return (((scol < BLOCK) & (scol > srow + prev_off))
                | ((scol >= BLOCK) & (scol - BLOCK <= srow)))

    def attend(b, hkv):
        qrows = slice(b * BLOCK, (b + 1) * BLOCK)
        krows = slice(b * BLOCK, (b + 2) * BLOCK)
        q2 = jnp.concatenate([q_ref[qrows, (2 * hkv) * LANES:(2 * hkv + 1) * LANES],
                              q_ref[qrows, (2 * hkv + 1) * LANES:(2 * hkv + 2) * LANES]], axis=0)
        kcat = jnp.concatenate([kpad_ref[2 * hkv, krows, :], kpad_ref[2 * hkv + 1, krows, :]], axis=0)
        s = lax.dot_general(q2, kcat, (((1,), (1,)), ((), ())), preferred_element_type=F32)
        s = jnp.where(band_mask(b), s, NEG_INF)
        sink_lo = jnp.where(row_first_slab, sinks_ref[4 * hkv], sinks_ref[4 * hkv + 2])
        sink_hi = jnp.where(row_first_slab, sinks_ref[4 * hkv + 1], sinks_ref[4 * hkv + 3])
        s_lo, s_hi = s[:, :2 * BLOCK], s[:, 2 * BLOCK:]
        m_lo = jnp.maximum(jnp.max(s_lo, axis=1, keepdims=True), sink_lo)
        m_hi = jnp.maximum(jnp.max(s_hi, axis=1, keepdims=True), sink_hi)
        p = jnp.concatenate([jnp.exp(s_lo - m_lo), jnp.exp(s_hi - m_hi)], axis=1).astype(BF16)
        vcat = jnp.concatenate(
            [jnp.concatenate([vpad_ref[2 * hkv, krows, :], ones_lo], axis=1),
             jnp.concatenate([vpad_ref[2 * hkv + 1, krows, :], ones_hi], axis=1)], axis=0)
        o2 = jnp.dot(p, vcat, preferred_element_type=F32)
        den = o2[:, LANES:] + jnp.where(lo_half, jnp.exp(sink_lo - m_lo), jnp.exp(sink_hi - m_hi))
        out = o2[:, :LANES] / den
        a_ref[qrows, (2 * hkv) * LANES:(2 * hkv + 1) * LANES] = out[:BLOCK]
        a_ref[qrows, (2 * hkv + 1) * LANES:(2 * hkv + 2) * LANES] = out[BLOCK:]

    def carry_band():
        last = slice(tb, tb + BLOCK)
        kpad_ref[:, :BLOCK, :] = kpad_ref[:, last, :]
        vpad_ref[:, :BLOCK, :] = vpad_ref[:, last, :]

    trow = lax.broadcasted_iota(jnp.int32, (BLOCK, BLOCK), 0)
    tcol = lax.broadcasted_iota(jnp.int32, (BLOCK, BLOCK), 1)
    causal = tcol <= trow

    def gmlp(c):
        gu = jax.nn.gelu(z_ref[:, col_gu + c * chunk:col_gu + (c + 1) * chunk])
        gv = jax.nn.gelu(z_ref[:, col_gv + c * chunk:col_gv + (c + 1) * chunk])
        gv = gv * lax.rsqrt(_group_sumsq(gv, bd_group) * (1.0 / GMLP_GROUP_DIM) + EPS)
        gv = (gv * gvg_ref[:, c * chunk:(c + 1) * chunk]).astype(BF16)
        for jj in range(2):
            grp = 2 * c + jj
            w = jnp.where(causal, ws_ref[grp], 0.0).astype(BF16)
            bias = bst_ref[:, grp:grp + 1]
            for b in range(n_blk):
                rws = slice(b * BLOCK, (b + 1) * BLOCK)
                sp = jnp.dot(w, gv[rws, jj * LANES:(jj + 1) * LANES], preferred_element_type=F32) + bias
                g_ref[rws, grp * LANES:(grp + 1) * LANES] = gu[rws, jj * LANES:(jj + 1) * LANES] * sp

    def a_norm():
        m_ref[:, :ATTN_WIDTH] = _rmsnorm_rows(a_ref[...], ag_ref[...]).astype(BF16)

    def g_norm():
        m_ref[:, ATTN_WIDTH:] = _rmsnorm_rows(g_ref[...], gg_ref[...]).astype(BF16)

    def out_a(n):
        cols = slice(n * chunk, (n + 1) * chunk)
        o_ref[:, cols] = x_ref[:, cols] + jnp.dot(m_ref[:, :ATTN_WIDTH], wout_ref[:ATTN_WIDTH, cols],
                                                  preferred_element_type=F32)

    def out_g(n):
        cols = slice(n * chunk, (n + 1) * chunk)
        o_ref[:, cols] += jnp.dot(m_ref[:, ATTN_WIDTH:], wout_ref[ATTN_WIDTH:, cols],
                                  preferred_element_type=F32)

    n_q = ATTN_WIDTH // chunk
    n_g = GMLP_WIDTH // chunk
    n_out = o_ref.shape[1] // chunk
    units = [(b, hkv) for b in range(n_blk) for hkv in range(N_KV_HEADS)]
    assert len(units) == 2 * n_g

    x_norm()
    proj(0)
    for c in range(1, n_q):
        proj(c * chunk)
        q_norm(c - 1)
    proj(col_k)
    q_norm(n_q - 1)
    proj(col_v)
    proj(col_gu)
    kv_place()
    proj(col_gv)
    for c in range(n_g):
        if c + 1 < n_g:
            proj(col_gu + (c + 1) * chunk)
        attend(*units[2 * c])
        if c + 1 < n_g:
            proj(col_gv + (c + 1) * chunk)
        attend(*units[2 * c + 1])
        gmlp(c)
    carry_band()
    a_norm()
    for n in range(n_out):
        out_a(n)
    g_norm()
    for n in range(n_out):
        out_g(n)


def _mixer(x, pos, invf, gm, w_in, qg, kg, sinks, gvg, w_s, bst, ag, gg, w_out, *, seq_len):
    t, d = x.shape
    tb = MIXER_TOKEN_TILE
    assert t % tb == 0 and seq_len % tb == 0 and tb % BLOCK == 0

    def resident(shape):
        return pl.BlockSpec(shape, lambda i: (0,) * len(shape), pipeline_mode=pl.Buffered(1))

    return pl.pallas_call(
        functools.partial(_mixer_kernel, tb=tb, tiles_per_seq=seq_len // tb),
        out_shape=jax.ShapeDtypeStruct((t, d), F32),
        grid=(t // tb,),
        in_specs=[
            pl.BlockSpec((tb, d), lambda i: (i, 0)),
            pl.BlockSpec((None, 1, tb), lambda i: (i, 0, 0)),
            resident(invf.shape),
            resident(gm.shape),
            resident(w_in.shape),
            resident(qg.shape),
            resident(kg.shape),
            pl.BlockSpec(memory_space=pltpu.SMEM),
            resident(gvg.shape),
            resident(w_s.shape),
            resident(bst.shape),
            resident(ag.shape),
            resident(gg.shape),
            resident(w_out.shape),
        ],
        out_specs=pl.BlockSpec((tb, d), lambda i: (i, 0)),
        scratch_shapes=[
            pltpu.VMEM((tb, d), BF16),
            pltpu.VMEM((tb, w_in.shape[1]), F32),
            pltpu.VMEM((tb, ATTN_WIDTH), BF16),
            pltpu.VMEM((2 * N_KV_HEADS, BLOCK + tb, LANES), BF16),
            pltpu.VMEM((2 * N_KV_HEADS, BLOCK + tb, LANES), BF16),
            pltpu.VMEM((tb, ATTN_WIDTH), F32),
            pltpu.VMEM((tb, GMLP_WIDTH), F32),
            pltpu.VMEM((tb, ATTN_WIDTH + GMLP_WIDTH), BF16),
        ],
        compiler_params=pltpu.CompilerParams(
            dimension_semantics=("arbitrary",),
            vmem_limit_bytes=V7X_VMEM_LIMIT_BYTES,
        ),
        name="mixer",
    )(x, pos, invf, gm, w_in, qg, kg, sinks, gvg, w_s, bst, ag, gg, w_out)


def kernel(x, positions, ffn1_norm, ffn1_w_gate, ffn1_w_up, ffn1_w_down, mix_norm, w_in,
           q_norm, k_norm, attn_sinks, gmlp_v_norm, gmlp_w_s, gmlp_b_s, attn_out_norm,
           gmlp_out_norm, w_out, ffn2_norm, ffn2_w_gate, ffn2_w_up, ffn2_w_down):
    b, s, d = x.shape
    depth = w_in.shape[0]
    xt = x.reshape(b * s, d)
    pos = positions.reshape(b * s // MIXER_TOKEN_TILE, 1, MIXER_TOKEN_TILE)
    invf = (ROPE_THETA ** (-jnp.arange(0, ROT_DIM, 2, dtype=F32) / ROT_DIM)).reshape(ROT_DIM // 2, 1)

    for l in range(depth):
        xt = _ffn(xt, ffn1_norm[l].reshape(1, d), ffn1_w_gate[l].astype(BF16),
                  ffn1_w_up[l].astype(BF16), ffn1_w_down[l].astype(BF16))
        xt = _mixer(
            xt, pos, invf, mix_norm[l].reshape(1, d), w_in[l].astype(BF16),
            (jnp.tile(q_norm[l], 2) * (HEAD_DIM ** -0.5)).reshape(1, LANES),
            jnp.tile(k_norm[l], 2).reshape(1, LANES),
            attn_sinks[l], gmlp_v_norm[l].reshape(1, GMLP_WIDTH), gmlp_w_s[l], gmlp_b_s[l].T,
            attn_out_norm[l].reshape(1, ATTN_WIDTH), gmlp_out_norm[l].reshape(1, GMLP_WIDTH),
            w_out[l].astype(BF16), seq_len=s)
        xt = _ffn(xt, ffn2_norm[l].reshape(1, d), ffn2_w_gate[l].astype(BF16),
                  ffn2_w_up[l].astype(BF16), ffn2_w_down[l].astype(BF16))
    return xt.reshape(b, s, d)
```

```python
import functools

import jax
import jax.numpy as jnp
from jax import lax
from jax.experimental import pallas as pl
from jax.experimental.pallas import tpu as pltpu

HEAD_DIM = 64
N_Q_HEADS = 16
N_KV_HEADS = 4
ATTN_WIDTH = N_Q_HEADS * HEAD_DIM
KV_WIDTH = N_KV_HEADS * HEAD_DIM
BLOCK = 128
ROT_DIM = HEAD_DIM // 4
ROPE_THETA = 500000.0
GMLP_GROUP_DIM = 128
GMLP_WIDTH = 1024
N_GMLP_GROUPS = GMLP_WIDTH // GMLP_GROUP_DIM
EPS = 1e-6
NEG_INF = -1e30

LANES = 128
V7X_VMEM_LIMIT_BYTES = 60000 * 1024

FFN_TOKEN_TILE = 1024
FFN_FF_TILE = 512
MIXER_TOKEN_TILE = 256

F32 = jnp.float32
BF16 = jnp.bfloat16


def _rmsnorm_rows(x, gain):
    y = x * lax.rsqrt(jnp.mean(x * x, axis=-1, keepdims=True) + EPS)
    return y * gain


def _ffn_kernel(x_ref, gain_ref, wg_ref, wu_ref, wd_ref, o_ref, h_ref):
    @pl.when(pl.program_id(1) == 0)
    def _():
        x = x_ref[...]
        h_ref[...] = _rmsnorm_rows(x, gain_ref[...]).astype(BF16)
        o_ref[...] = x

    h = h_ref[...]
    g = jnp.dot(h, wg_ref[...], preferred_element_type=F32)
    u = jnp.dot(h, wu_ref[...], preferred_element_type=F32)
    a = (jax.nn.silu(g) * (0.5 * u)).astype(BF16)
    o_ref[...] += jnp.dot(a, wd_ref[...], preferred_element_type=F32)


def _ffn(x, gain, w_gate, w_up, w_down):
    t, d = x.shape
    d_ff = w_gate.shape[1]
    tm, tf = FFN_TOKEN_TILE, FFN_FF_TILE
    assert t % tm == 0 and d_ff % tf == 0
    return pl.pallas_call(
        _ffn_kernel,
        out_shape=jax.ShapeDtypeStruct((t, d), F32),
        grid=(t // tm, d_ff // tf),
        in_specs=[
            pl.BlockSpec((tm, d), lambda i, f: (i, 0)),
            pl.BlockSpec((1, d), lambda i, f: (0, 0)),
            pl.BlockSpec((d, tf), lambda i, f: (0, f)),
            pl.BlockSpec((d, tf), lambda i, f: (0, f)),
            pl.BlockSpec((tf, d), lambda i, f: (f, 0)),
        ],
        out_specs=pl.BlockSpec((tm, d), lambda i, f: (i, 0)),
        scratch_shapes=[pltpu.VMEM((tm, d), BF16)],
        compiler_params=pltpu.CompilerParams(
            dimension_semantics=("arbitrary", "arbitrary"),
            vmem_limit_bytes=V7X_VMEM_LIMIT_BYTES,
        ),
        name="ffn",
    )(x, gain, w_gate, w_up, w_down)


def _block_diag_ones(size, group):
    r = lax.broadcasted_iota(jnp.int32, (size, size), 0) // group
    c = lax.broadcasted_iota(jnp.int32, (size, size), 1) // group
    return jnp.where(r == c, 1.0, 0.0).astype(BF16)


def _mixer_kernel(xa_ref, x_ref, pos_ref, invf_ref, gm_ref, win_ref, qg_ref, kg_ref, sinks_ref,
                  gvg_ref, ws_ref, bst_ref, ag_ref, gg_ref, wout_ref, o_ref,
                  h_ref, z_ref, q_ref, kpad_ref, vpad_ref, a_ref, g_ref, m_ref,
                  *, tb, tiles_per_seq):
    i = pl.program_id(0)
    n_blk = tb // BLOCK
    chunk = 2 * LANES
    half = ROT_DIM // 2
    col_k = ATTN_WIDTH
    col_v = col_k + KV_WIDTH
    col_gu = col_v + KV_WIDTH
    col_gv = col_gu + GMLP_WIDTH

    @pl.when(i == 0)
    def _():
        z_ref[...] = jnp.zeros_like(z_ref)
        kpad_ref[:, :BLOCK, :] = jnp.zeros((2 * N_KV_HEADS, BLOCK, LANES), BF16)
        vpad_ref[:, :BLOCK, :] = jnp.zeros((2 * N_KV_HEADS, BLOCK, LANES), BF16)

    lane = lax.broadcasted_iota(jnp.int32, (1, LANES), 1)
    lo_half = lane < HEAD_DIM

    ang = invf_ref[...] * pos_ref[...].astype(F32)
    cos, sin = jnp.cos(ang), jnp.sin(ang)
    rest_one = jnp.ones((HEAD_DIM - ROT_DIM, tb), F32)
    rest_zero = jnp.zeros((HEAD_DIM - ROT_DIM, tb), F32)
    zero = jnp.zeros((half, tb), F32)
    rope_c = jnp.concatenate([cos, cos, rest_one] * 2, axis=0).T
    rope_s1 = jnp.concatenate([-sin, zero, rest_zero] * 2, axis=0).T
    rope_s2 = jnp.concatenate([zero, sin, rest_zero] * 2, axis=0).T

    bd_head = _block_diag_ones(chunk, HEAD_DIM)
    bd_group = _block_diag_ones(chunk, GMLP_GROUP_DIM)

    def rope(t):
        return t * rope_c + pltpu.roll(t, LANES - half, 1) * rope_s1 + pltpu.roll(t, half, 1) * rope_s2

    def x_norm():
        h_ref[...] = _rmsnorm_rows(xa_ref[...], gm_ref[...]).astype(BF16)

    def proj(c0):
        cols = slice(c0, c0 + chunk)
        z_ref[:, cols] = jnp.dot(h_ref[...], win_ref[:, cols], preferred_element_type=F32)

    def heads_load(c0):
        z = z_ref[:, c0:c0 + chunk]
        return z, (z * z).astype(BF16)

    def heads_dot(sq):
        return jnp.dot(sq, bd_head, preferred_element_type=F32)

    def heads_finish(z, ss, gain):
        zn = z * lax.rsqrt(ss * (1.0 / HEAD_DIM) + EPS)
        return [rope(zn[:, j * LANES:(j + 1) * LANES] * gain) for j in range(2)]

    def q_finish(c, z, ss):
        s0, s1 = heads_finish(z, ss, qg_ref[...])
        q_ref[:, c * chunk:(c + 1) * chunk] = jnp.concatenate([s0, s1], axis=1).astype(BF16)

    def place_heads(slab, dst_ref, j):
        rolled = pltpu.roll(slab, HEAD_DIM, 1)
        zero_slab = jnp.zeros_like(slab)
        cur = slice(BLOCK, BLOCK + tb)
        dst_ref[4 * j + 0, cur, :] = jnp.where(lo_half, slab, zero_slab).astype(BF16)
        dst_ref[4 * j + 1, cur, :] = jnp.where(lo_half, zero_slab, rolled).astype(BF16)
        dst_ref[4 * j + 2, cur, :] = jnp.where(lo_half, rolled, zero_slab).astype(BF16)
        dst_ref[4 * j + 3, cur, :] = jnp.where(lo_half, zero_slab, slab).astype(BF16)

    def kv_finish(zk, ss, zv):
        k_slabs = heads_finish(zk, ss, kg_ref[...])
        for j in range(2):
            place_heads(k_slabs[j], kpad_ref, j)
            place_heads(zv[:, j * LANES:(j + 1) * LANES], vpad_ref, j)

    srow = lax.broadcasted_iota(jnp.int32, (2 * BLOCK, 4 * BLOCK), 0) % BLOCK
    scol = lax.broadcasted_iota(jnp.int32, (2 * BLOCK, 4 * BLOCK), 1) % (2 * BLOCK)
    row_first_slab = lax.broadcasted_iota(jnp.int32, (2 * BLOCK, 1), 0) < BLOCK
    seq_start = ((i + tiles_per_seq - 1) % tiles_per_seq) == 0
    ones_lo = jnp.broadcast_to(jnp.where(lo_half, 1.0, 0.0).astype(BF16), (2 * BLOCK, LANES))
    ones_hi = jnp.broadcast_to(jnp.where(lo_half, 0.0, 1.0).astype(BF16), (2 * BLOCK, LANES))

    def band_mask(b):
        prev_off = jnp.where(seq_start, BLOCK, 0) if b == 0 else 0
        return (((scol < BLOCK) & (scol > srow + prev_off))
                | ((scol >= BLOCK) & (scol - BLOCK <= srow)))

    def att_scores(b, hkv):
        qrows = slice(b * BLOCK, (b + 1) * BLOCK)
        krows = slice(b * BLOCK, (b + 2) * BLOCK)
        q2 = jnp.concatenate([q_ref[qrows, (2 * hkv) * LANES:(2 * hkv + 1) * LANES],
                              q_ref[qrows, (2 * hkv + 1) * LANES:(2 * hkv + 2) * LANES]], axis=0)
        kcat = jnp.concatenate([kpad_ref[2 * hkv, krows, :], kpad_ref[2 * hkv + 1, krows, :]], axis=0)
        return lax.dot_general(q2, kcat, (((1,), (1,)), ((), ())), preferred_element_type=F32)

    def att_softmax(b, hkv, s):
        s = jnp.where(band_mask(b), s, NEG_INF)
        sink_lo = jnp.where(row_first_slab, sinks_ref[4 * hkv], sinks_ref[4 * hkv + 2])
        sink_hi = jnp.where(row_first_slab, sinks_ref[4 * hkv + 1], sinks_ref[4 * hkv + 3])
        s_lo, s_hi = s[:, :2 * BLOCK], s[:, 2 * BLOCK:]
        m_lo = jnp.maximum(jnp.max(s_lo, axis=1, keepdims=True), sink_lo)
        m_hi = jnp.maximum(jnp.max(s_hi, axis=1, keepdims=True), sink_hi)
        p = jnp.concatenate([jnp.exp(s_lo - m_lo), jnp.exp(s_hi - m_hi)], axis=1).astype(BF16)
        sink_mass = jnp.where(lo_half, jnp.exp(sink_lo - m_lo), jnp.exp(sink_hi - m_hi))
        return p, sink_mass

    def att_values(b, hkv, p):
        krows = slice(b * BLOCK, (b + 2) * BLOCK)
        vcat = jnp.concatenate(
            [jnp.concatenate([vpad_ref[2 * hkv, krows, :], ones_lo], axis=1),
             jnp.concatenate([vpad_ref[2 * hkv + 1, krows, :], ones_hi], axis=1)], axis=0)
        return jnp.dot(p, vcat, preferred_element_type=F32)

    def att_finish(b, hkv, o2, sink_mass):
        qrows = slice(b * BLOCK, (b + 1) * BLOCK)
        out = o2[:, :LANES] / (o2[:, LANES:] + sink_mass)
        a_ref[qrows, (2 * hkv) * LANES:(2 * hkv + 1) * LANES] = out[:BLOCK]
        a_ref[qrows, (2 * hkv + 1) * LANES:(2 * hkv + 2) * LANES] = out[BLOCK:]

    def carry_band():
        last = slice(tb, tb + BLOCK)
        kpad_ref[:, :BLOCK, :] = kpad_ref[:, last, :]
        vpad_ref[:, :BLOCK, :] = vpad_ref[:, last, :]

    trow = lax.broadcasted_iota(jnp.int32, (BLOCK, BLOCK), 0)
    tcol = lax.broadcasted_iota(jnp.int32, (BLOCK, BLOCK), 1)
    causal = tcol <= trow

    def gmlp_act(c):
        gu = jax.nn.gelu(z_ref[:, col_gu + c * chunk:col_gu + (c + 1) * chunk])
        gv = jax.nn.gelu(z_ref[:, col_gv + c * chunk:col_gv + (c + 1) * chunk])
        return gu, gv, (gv * gv).astype(BF16)

    def gmlp_norm_dot(sq):
        return jnp.dot(sq, bd_group, preferred_element_type=F32)

    def gmlp_scale(c, gv, ss):
        gv = gv * lax.rsqrt(ss * (1.0 / GMLP_GROUP_DIM) + EPS)
        return (gv * gvg_ref[:, c * chunk:(c + 1) * chunk]).astype(BF16)

    def gmlp_spatial_dots(c, gvn):
        sps = []
        for jj in range(2):
            w = jnp.where(causal, ws_ref[2 * c + jj], 0.0).astype(BF16)
            for b in range(n_blk):
                rws = slice(b * BLOCK, (b + 1) * BLOCK)
                sps.append(jnp.dot(w, gvn[rws, jj * LANES:(jj + 1) * LANES], preferred_element_type=F32))
        return sps

    def gmlp_finish(c, gu, sps):
        for jj in range(2):
            grp = 2 * c + jj
            bias = bst_ref[:, grp:grp + 1]
            for b in range(n_blk):
                rws = slice(b * BLOCK, (b + 1) * BLOCK)
                g_ref[rws, grp * LANES:(grp + 1) * LANES] = (
                    gu[rws, jj * LANES:(jj + 1) * LANES] * (sps[jj * n_blk + b] + bias))

    def a_norm():
        m_ref[:, :ATTN_WIDTH] = _rmsnorm_rows(a_ref[...], ag_ref[...]).astype(BF16)

    def g_norm():
        m_ref[:, ATTN_WIDTH:] = _rmsnorm_rows(g_ref[...], gg_ref[...]).astype(BF16)

    def out_a(n):
        cols = slice(n * chunk, (n + 1) * chunk)
        o_ref[:, cols] = x_ref[:, cols] + jnp.dot(m_ref[:, :ATTN_WIDTH], wout_ref[:ATTN_WIDTH, cols],
                                                  preferred_element_type=F32)

    def out_g(n):
        cols = slice(n * chunk, (n + 1) * chunk)
        o_ref[:, cols] += jnp.dot(m_ref[:, ATTN_WIDTH:], wout_ref[ATTN_WIDTH:, cols],
                                  preferred_element_type=F32)

    n_q = ATTN_WIDTH // chunk
    n_g = GMLP_WIDTH // chunk
    n_out = o_ref.shape[1] // chunk
    units = [(b, hkv) for b in range(n_blk) for hkv in range(N_KV_HEADS)]
    assert len(units) == 2 * n_g

    x_norm()
    q_in = [heads_load(c * chunk) for c in range(n_q)]
    zk, k_sq = heads_load(col_k)
    zv = z_ref[:, col_v:col_v + KV_WIDTH]
    proj(0)
    q_ss = [heads_dot(sq) for _, sq in q_in]
    k_ss = heads_dot(k_sq)
    for c in range(n_q):
        q_finish(c, q_in[c][0], q_ss[c])
        if c % 2 == 1:
            proj((c + 1) // 2 * chunk)
    kv_finish(zk, k_ss, zv)
    proj((n_q - 1) * chunk)

    big = [col_k, col_v]
    for c in range(n_g):
        big += [col_gu + c * chunk, col_gv + c * chunk]
    scores, probs, act, scaled, spatial = {}, {}, {}, {}, {}
    for r in range(len(units) + 1):
        c = r // 2
        if r >= 1:
            weighted = att_values(*units[r - 1], probs[r - 1][0])
        if r < len(units):
            scores[r] = att_scores(*units[r])
        if r % 2 == 1:
            norm_ss = gmlp_norm_dot(act[c][2])
        elif c >= 1:
            spatial[c - 1] = gmlp_spatial_dots(c - 1, scaled[c - 1])
        if r >= 1:
            att_finish(*units[r - 1], weighted, probs[r - 1][1])
        if r < len(units):
            probs[r] = att_softmax(*units[r], scores[r])
        if r % 2 == 1:
            scaled[c] = gmlp_scale(c, act[c][1], norm_ss)
        else:
            if c >= 1:
                gmlp_finish(c - 1, act[c - 1][0], spatial[c - 1])
            if c < n_g:
                act[c] = gmlp_act(c)
        proj(big[r])
    carry_band()
    a_norm()
    proj(big[-1])
    for n in range(n_out):
        out_a(n)
    g_norm()
    for n in range(n_out):
        out_g(n)


def _mixer(x, pos, invf, gm, w_in, qg, kg, sinks, gvg, w_s, bst, ag, gg, w_out, *, seq_len):
    t, d = x.shape
    tb = MIXER_TOKEN_TILE
    assert t % tb == 0 and seq_len % tb == 0 and tb % BLOCK == 0
    n_tiles = t // tb

    def resident(shape):
        return pl.BlockSpec(shape, lambda i: (0,) * len(shape), pipeline_mode=pl.Buffered(1))

    def first_stage_tile(i):
        return jnp.minimum(i, n_tiles - 1)

    def second_stage_tile(i):
        return jnp.maximum(i - 1, 0)

    return pl.pallas_call(
        functools.partial(_mixer_kernel, tb=tb, tiles_per_seq=seq_len // tb),
        out_shape=jax.ShapeDtypeStruct((t, d), F32),
        grid=(n_tiles + 1,),
        in_specs=[
            pl.BlockSpec((tb, d), lambda i: (first_stage_tile(i), 0)),
            pl.BlockSpec((tb, d), lambda i: (second_stage_tile(i), 0)),
            pl.BlockSpec((None, 1, tb), lambda i: (second_stage_tile(i), 0, 0)),
            resident(invf.shape),
            resident(gm.shape),
            resident(w_in.shape),
            resident(qg.shape),
            resident(kg.shape),
            pl.BlockSpec(memory_space=pltpu.SMEM),
            resident(gvg.shape),
            resident(w_s.shape),
            resident(bst.shape),
            resident(ag.shape),
            resident(gg.shape),
            resident(w_out.shape),
        ],
        out_specs=pl.BlockSpec((tb, d), lambda i: (second_stage_tile(i), 0)),
        scratch_shapes=[
            pltpu.VMEM((tb, d), BF16),
            pltpu.VMEM((tb, w_in.shape[1]), F32),
            pltpu.VMEM((tb, ATTN_WIDTH), BF16),
            pltpu.VMEM((2 * N_KV_HEADS, BLOCK + tb, LANES), BF16),
            pltpu.VMEM((2 * N_KV_HEADS, BLOCK + tb, LANES), BF16),
            pltpu.VMEM((tb, ATTN_WIDTH), F32),
            pltpu.VMEM((tb, GMLP_WIDTH), F32),
            pltpu.VMEM((tb, ATTN_WIDTH + GMLP_WIDTH), BF16),
        ],
        compiler_params=pltpu.CompilerParams(
            dimension_semantics=("arbitrary",),
            vmem_limit_bytes=V7X_VMEM_LIMIT_BYTES,
        ),
        name="mixer",
    )(x, x, pos, invf, gm, w_in, qg, kg, sinks, gvg, w_s, bst, ag, gg, w_out)


def kernel(x, positions, ffn1_norm, ffn1_w_gate, ffn1_w_up, ffn1_w_down, mix_norm, w_in,
           q_norm, k_norm, attn_sinks, gmlp_v_norm, gmlp_w_s, gmlp_b_s, attn_out_norm,
           gmlp_out_norm, w_out, ffn2_norm, ffn2_w_gate, ffn2_w_up, ffn2_w_down):
    b, s, d = x.shape
    depth = w_in.shape[0]
    xt = x.reshape(b * s, d)
    pos = positions.reshape(b * s // MIXER_TOKEN_TILE, 1, MIXER_TOKEN_TILE)
    invf = (ROPE_THETA ** (-jnp.arange(0, ROT_DIM, 2, dtype=F32) / ROT_DIM)).reshape(ROT_DIM // 2, 1)

    for l in range(depth):
        xt = _ffn(xt, ffn1_norm[l].reshape(1, d), ffn1_w_gate[l].astype(BF16),
                  ffn1_w_up[l].astype(BF16), ffn1_w_down[l].astype(BF16))
        xt = _mixer(
            xt, pos, invf, mix_norm[l].reshape(1, d), w_in[l].astype(BF16),
            (jnp.tile(q_norm[l], 2) * (HEAD_DIM ** -0.5)).reshape(1, LANES),
            jnp.tile(k_norm[l], 2).reshape(1, LANES),
            attn_sinks[l], gmlp_v_norm[l].reshape(1, GMLP_WIDTH), gmlp_w_s[l], gmlp_b_s[l].T,
            attn_out_norm[l].reshape(1, ATTN_WIDTH), gmlp_out_norm[l].reshape(1, GMLP_WIDTH),
            w_out[l].astype(BF16), seq_len=s)
        xt = _ffn(xt, ffn2_norm[l].reshape(1, d), ffn2_w_gate[l].astype(BF16),
                  ffn2_w_up[l].astype(BF16), ffn2_w_down[l].astype(BF16))
    return xt.reshape(b, s, d)
```

```python
import functools

import jax
import jax.numpy as jnp
from jax import lax
from jax.experimental import pallas as pl
from jax.experimental.pallas import tpu as pltpu

HEAD_DIM = 64
N_Q_HEADS = 16
N_KV_HEADS = 4
ATTN_WIDTH = N_Q_HEADS * HEAD_DIM
KV_WIDTH = N_KV_HEADS * HEAD_DIM
BLOCK = 128
ROT_DIM = HEAD_DIM // 4
ROPE_THETA = 500000.0
GMLP_GROUP_DIM = 128
GMLP_WIDTH = 1024
N_GMLP_GROUPS = GMLP_WIDTH // GMLP_GROUP_DIM
EPS = 1e-6
NEG_INF = -1e30

LANES = 128
V7X_VMEM_LIMIT_BYTES = 60000 * 1024

FFN_TOKEN_TILE = 1024
FFN_FF_TILE = 512
MIXER_TOKEN_TILE = 256

F32 = jnp.float32
BF16 = jnp.bfloat16


def _rmsnorm_rows(x, gain):
    y = x * lax.rsqrt(jnp.mean(x * x, axis=-1, keepdims=True) + EPS)
    return y * gain


def _ffn_kernel(x_ref, gain_ref, wg_ref, wu_ref, wd_ref, o_ref, h_ref):
    @pl.when(pl.program_id(1) == 0)
    def _():
        x = x_ref[...]
        h_ref[...] = _rmsnorm_rows(x, gain_ref[...]).astype(BF16)
        o_ref[...] = x

    h = h_ref[...]
    g = jnp.dot(h, wg_ref[...], preferred_element_type=F32)
    u = jnp.dot(h, wu_ref[...], preferred_element_type=F32)
    a = (jax.nn.silu(g) * (0.5 * u)).astype(BF16)
    o_ref[...] += jnp.dot(a, wd_ref[...], preferred_element_type=F32)


def _ffn(x, gain, w_gate, w_up, w_down):
    t, d = x.shape
    d_ff = w_gate.shape[1]
    tm, tf = FFN_TOKEN_TILE, FFN_FF_TILE
    assert t % tm == 0 and d_ff % tf == 0
    return pl.pallas_call(
        _ffn_kernel,
        out_shape=jax.ShapeDtypeStruct((t, d), F32),
        grid=(t // tm, d_ff // tf),
        in_specs=[
            pl.BlockSpec((tm, d), lambda i, f: (i, 0)),
            pl.BlockSpec((1, d), lambda i, f: (0, 0)),
            pl.BlockSpec((d, tf), lambda i, f: (0, f)),
            pl.BlockSpec((d, tf), lambda i, f: (0, f)),
            pl.BlockSpec((tf, d), lambda i, f: (f, 0)),
        ],
        out_specs=pl.BlockSpec((tm, d), lambda i, f: (i, 0)),
        scratch_shapes=[pltpu.VMEM((tm, d), BF16)],
        compiler_params=pltpu.CompilerParams(
            dimension_semantics=("arbitrary", "arbitrary"),
            vmem_limit_bytes=V7X_VMEM_LIMIT_BYTES,
        ),
        name="ffn",
    )(x, gain, w_gate, w_up, w_down)


def _block_diag_ones(size, group):
    r = lax.broadcasted_iota(jnp.int32, (size, size), 0) // group
    c = lax.broadcasted_iota(jnp.int32, (size, size), 1) // group
    return jnp.where(r == c, 1.0, 0.0).astype(BF16)


def _mixer_kernel(xa_ref, pos_ref, invf_ref, gm_ref, win_ref, qg_ref, kg_ref, sinks_ref,
                  gvg_ref, ws_ref, bst_ref, ag_ref, gg_ref, wout_ref, o_ref,
                  h_ref, z_ref, q_ref, kpad_ref, vpad_ref, a_ref, g_ref, m_ref, x_ref,
                  *, tb, tiles_per_seq):
    i = pl.program_id(0)
    n_blk = tb // BLOCK
    chunk = 2 * LANES
    half = ROT_DIM // 2
    col_k = ATTN_WIDTH
    col_v = col_k + KV_WIDTH
    col_gu = col_v + KV_WIDTH
    col_gv = col_gu + GMLP_WIDTH

    @pl.when(i == 0)
    def _():
        z_ref[...] = jnp.zeros_like(z_ref)
        x_ref[...] = jnp.zeros_like(x_ref)
        kpad_ref[:, :BLOCK, :] = jnp.zeros((2 * N_KV_HEADS, BLOCK, LANES), BF16)
        vpad_ref[:, :BLOCK, :] = jnp.zeros((2 * N_KV_HEADS, BLOCK, LANES), BF16)

    lane = lax.broadcasted_iota(jnp.int32, (1, LANES), 1)
    lo_half = lane < HEAD_DIM

    ang = invf_ref[...] * pos_ref[jnp.maximum(i - 1, 0)].astype(F32)
    cos, sin = jnp.cos(ang), jnp.sin(ang)
    rest_one = jnp.ones((HEAD_DIM - ROT_DIM, tb), F32)
    rest_zero = jnp.zeros((HEAD_DIM - ROT_DIM, tb), F32)
    zero = jnp.zeros((half, tb), F32)
    rope_c = jnp.concatenate([cos, cos, rest_one] * 2, axis=0).T
    rope_s1 = jnp.concatenate([-sin, zero, rest_zero] * 2, axis=0).T
    rope_s2 = jnp.concatenate([zero, sin, rest_zero] * 2, axis=0).T

    bd_head = _block_diag_ones(chunk, HEAD_DIM)
    bd_group = _block_diag_ones(chunk, GMLP_GROUP_DIM)

    def rope(t):
        return t * rope_c + pltpu.roll(t, LANES - half, 1) * rope_s1 + pltpu.roll(t, half, 1) * rope_s2

    def x_norm():
        h_ref[...] = _rmsnorm_rows(xa_ref[...], gm_ref[...]).astype(BF16)

    def proj(c0):
        cols = slice(c0, c0 + chunk)
        z_ref[:, cols] = jnp.dot(h_ref[...], win_ref[:, cols], preferred_element_type=F32)

    def heads_load(c0):
        z = z_ref[:, c0:c0 + chunk]
        return z, (z * z).astype(BF16)

    def heads_dot(sq):
        return jnp.dot(sq, bd_head, preferred_element_type=F32)

    def heads_finish(z, ss, gain):
        zn = z * lax.rsqrt(ss * (1.0 / HEAD_DIM) + EPS)
        return [rope(zn[:, j * LANES:(j + 1) * LANES] * gain) for j in range(2)]

    def q_finish(c, z, ss):
        s0, s1 = heads_finish(z, ss, qg_ref[...])
        q_ref[:, c * chunk:(c + 1) * chunk] = jnp.concatenate([s0, s1], axis=1).astype(BF16)

    def place_heads(slab, dst_ref, j):
        rolled = pltpu.roll(slab, HEAD_DIM, 1)
        zero_slab = jnp.zeros_like(slab)
        cur = slice(BLOCK, BLOCK + tb)
        dst_ref[4 * j + 0, cur, :] = jnp.where(lo_half, slab, zero_slab).astype(BF16)
        dst_ref[4 * j + 1, cur, :] = jnp.where(lo_half, zero_slab, rolled).astype(BF16)
        dst_ref[4 * j + 2, cur, :] = jnp.where(lo_half, rolled, zero_slab).astype(BF16)
        dst_ref[4 * j + 3, cur, :] = jnp.where(lo_half, zero_slab, slab).astype(BF16)

    def kv_finish(zk, ss, zv):
        k_slabs = heads_finish(zk, ss, kg_ref[...])
        for j in range(2):
            place_heads(k_slabs[j], kpad_ref, j)
            place_heads(zv[:, j * LANES:(j + 1) * LANES], vpad_ref, j)

    srow = lax.broadcasted_iota(jnp.int32, (2 * BLOCK, 4 * BLOCK), 0) % BLOCK
    scol = lax.broadcasted_iota(jnp.int32, (2 * BLOCK, 4 * BLOCK), 1) % (2 * BLOCK)
    row_first_slab = lax.broadcasted_iota(jnp.int32, (2 * BLOCK, 1), 0) < BLOCK
    seq_start = ((i + tiles_per_seq - 1) % tiles_per_seq) == 0
    ones_lo = jnp.broadcast_to(jnp.where(lo_half, 1.0, 0.0).astype(BF16), (2 * BLOCK, LANES))
    ones_hi = jnp.broadcast_to(jnp.where(lo_half, 0.0, 1.0).astype(BF16), (2 * BLOCK, LANES))

    def band_mask(b):
        prev_off = jnp.where(seq_start, BLOCK, 0) if b == 0 else 0
        return (((scol < BLOCK) & (scol > srow + prev_off))
                | ((scol >= BLOCK) & (scol - BLOCK <= srow)))

    def att_scores(b, hkv):
        qrows = slice(b * BLOCK, (b + 1) * BLOCK)
        krows = slice(b * BLOCK, (b + 2) * BLOCK)
        q2 = jnp.concatenate([q_ref[qrows, (2 * hkv) * LANES:(2 * hkv + 1) * LANES],
                              q_ref[qrows, (2 * hkv + 1) * LANES:(2 * hkv + 2) * LANES]], axis=0)
        kcat = jnp.concatenate([kpad_ref[2 * hkv, krows, :], kpad_ref[2 * hkv + 1, krows, :]], axis=0)
        return lax.dot_general(q2, kcat, (((1,), (1,)), ((), ())), preferred_element_type=F32)

    def att_softmax(b, hkv, s):
        s = jnp.where(band_mask(b), s, NEG_INF)
        sink_lo = jnp.where(row_first_slab, sinks_ref[4 * hkv], sinks_ref[4 * hkv + 2])
        sink_hi = jnp.where(row_first_slab, sinks_ref[4 * hkv + 1], sinks_ref[4 * hkv + 3])
        s_lo, s_hi = s[:, :2 * BLOCK], s[:, 2 * BLOCK:]
        m_lo = jnp.maximum(jnp.max(s_lo, axis=1, keepdims=True), sink_lo)
        m_hi = jnp.maximum(jnp.max(s_hi, axis=1, keepdims=True), sink_hi)
        p = jnp.concatenate([jnp.exp(s_lo - m_lo), jnp.exp(s_hi - m_hi)], axis=1).astype(BF16)
        sink_mass = jnp.where(lo_half, jnp.exp(sink_lo - m_lo), jnp.exp(sink_hi - m_hi))
        return p, sink_mass

    def att_values(b, hkv, p):
        krows = slice(b * BLOCK, (b + 2) * BLOCK)
        vcat = jnp.concatenate(
            [jnp.concatenate([vpad_ref[2 * hkv, krows, :], ones_lo], axis=1),
             jnp.concatenate([vpad_ref[2 * hkv + 1, krows, :], ones_hi], axis=1)], axis=0)
        return jnp.dot(p, vcat, preferred_element_type=F32)

    def att_finish(b, hkv, o2, sink_mass):
        qrows = slice(b * BLOCK, (b + 1) * BLOCK)
        out = o2[:, :LANES] / (o2[:, LANES:] + sink_mass)
        a_ref[qrows, (2 * hkv) * LANES:(2 * hkv + 1) * LANES] = out[:BLOCK]
        a_ref[qrows, (2 * hkv + 1) * LANES:(2 * hkv + 2) * LANES] = out[BLOCK:]

    def carry_band():
        last = slice(tb, tb + BLOCK)
        kpad_ref[:, :BLOCK, :] = kpad_ref[:, last, :]
        vpad_ref[:, :BLOCK, :] = vpad_ref[:, last, :]

    trow = lax.broadcasted_iota(jnp.int32, (BLOCK, BLOCK), 0)
    tcol = lax.broadcasted_iota(jnp.int32, (BLOCK, BLOCK), 1)
    causal = tcol <= trow

    def gmlp_act(c):
        gu = jax.nn.gelu(z_ref[:, col_gu + c * chunk:col_gu + (c + 1) * chunk])
        gv = jax.nn.gelu(z_ref[:, col_gv + c * chunk:col_gv + (c + 1) * chunk])
        return gu, gv, (gv * gv).astype(BF16)

    def gmlp_norm_dot(sq):
        return jnp.dot(sq, bd_group, preferred_element_type=F32)

    def gmlp_scale(c, gv, ss):
        gv = gv * lax.rsqrt(ss * (1.0 / GMLP_GROUP_DIM) + EPS)
        return (gv * gvg_ref[:, c * chunk:(c + 1) * chunk]).astype(BF16)

    def gmlp_spatial_dots(c, gvn):
        sps = []
        for jj in range(2):
            w = jnp.where(causal, ws_ref[2 * c + jj], 0.0).astype(BF16)
            for b in range(n_blk):
                rws = slice(b * BLOCK, (b + 1) * BLOCK)
                sps.append(jnp.dot(w, gvn[rws, jj * LANES:(jj + 1) * LANES], preferred_element_type=F32))
        return sps

    def gmlp_finish(c, gu, sps):
        for jj in range(2):
            grp = 2 * c + jj
            bias = bst_ref[:, grp:grp + 1]
            for b in range(n_blk):
                rws = slice(b * BLOCK, (b + 1) * BLOCK)
                g_ref[rws, grp * LANES:(grp + 1) * LANES] = (
                    gu[rws, jj * LANES:(jj + 1) * LANES] * (sps[jj * n_blk + b] + bias))

    def a_norm():
        m_ref[:, :ATTN_WIDTH] = _rmsnorm_rows(a_ref[...], ag_ref[...]).astype(BF16)

    def g_norm():
        m_ref[:, ATTN_WIDTH:] = _rmsnorm_rows(g_ref[...], gg_ref[...]).astype(BF16)

    def out_a(n):
        cols = slice(n * chunk, (n + 1) * chunk)
        o_ref[:, cols] = x_ref[:, cols] + jnp.dot(m_ref[:, :ATTN_WIDTH], wout_ref[:ATTN_WIDTH, cols],
                                                  preferred_element_type=F32)
        x_ref[:, cols] = xa_ref[:, cols]

    def out_g(n):
        cols = slice(n * chunk, (n + 1) * chunk)
        o_ref[:, cols] += jnp.dot(m_ref[:, ATTN_WIDTH:], wout_ref[ATTN_WIDTH:, cols],
                                  preferred_element_type=F32)

    n_q = ATTN_WIDTH // chunk
    n_g = GMLP_WIDTH // chunk
    n_out = o_ref.shape[1] // chunk
    units = [(b, hkv) for b in range(n_blk) for hkv in range(N_KV_HEADS)]
    assert len(units) == 2 * n_g

    x_norm()
    q_in = [heads_load(c * chunk) for c in range(n_q)]
    zk, k_sq = heads_load(col_k)
    zv = z_ref[:, col_v:col_v + KV_WIDTH]
    proj(0)
    q_ss = [heads_dot(sq) for _, sq in q_in]
    k_ss = heads_dot(k_sq)
    for c in range(n_q):
        q_finish(c, q_in[c][0], q_ss[c])
        if c % 2 == 1:
            proj((c + 1) // 2 * chunk)
    kv_finish(zk, k_ss, zv)
    proj((n_q - 1) * chunk)

    big = [col_k, col_v]
    for c in range(n_g):
        big += [col_gu + c * chunk, col_gv + c * chunk]
    scores, probs, act, scaled, spatial = {}, {}, {}, {}, {}
    for r in range(len(units) + 1):
        c = r // 2
        if r >= 1:
            weighted = att_values(*units[r - 1], probs[r - 1][0])
        if r < len(units):
            scores[r] = att_scores(*units[r])
        if r % 2 == 1:
            norm_ss = gmlp_norm_dot(act[c][2])
        elif c >= 1:
            spatial[c - 1] = gmlp_spatial_dots(c - 1, scaled[c - 1])
        if r >= 1:
            att_finish(*units[r - 1], weighted, probs[r - 1][1])
        if r < len(units):
            probs[r] = att_softmax(*units[r], scores[r])
        if r % 2 == 1:
            scaled[c] = gmlp_scale(c, act[c][1], norm_ss)
        else:
            if c >= 1:
                gmlp_finish(c - 1, act[c - 1][0], spatial[c - 1])
            if c < n_g:
                act[c] = gmlp_act(c)
        proj(big[r])
    carry_band()
    a_norm()
    proj(big[-1])
    for n in range(n_out):
        out_a(n)
    g_norm()
    for n in range(n_out):
        out_g(n)


def _mixer(x, pos, invf, gm, w_in, qg, kg, sinks, gvg, w_s, bst, ag, gg, w_out, *, seq_len):
    t, d = x.shape
    tb = MIXER_TOKEN_TILE
    assert t % tb == 0 and seq_len % tb == 0 and tb % BLOCK == 0
    n_tiles = t // tb

    def resident(shape):
        return pl.BlockSpec(shape, lambda i: (0,) * len(shape), pipeline_mode=pl.Buffered(1))

    def first_stage_tile(i):
        return jnp.minimum(i, n_tiles - 1)

    def second_stage_tile(i):
        return jnp.maximum(i - 1, 0)

    return pl.pallas_call(
        functools.partial(_mixer_kernel, tb=tb, tiles_per_seq=seq_len // tb),
        out_shape=jax.ShapeDtypeStruct((t, d), F32),
        grid=(n_tiles + 1,),
        in_specs=[
            pl.BlockSpec((tb, d), lambda i: (first_stage_tile(i), 0)),
            resident(pos.shape),
            resident(invf.shape),
            resident(gm.shape),
            resident(w_in.shape),
            resident(qg.shape),
            resident(kg.shape),
            pl.BlockSpec(memory_space=pltpu.SMEM),
            resident(gvg.shape),
            resident(w_s.shape),
            resident(bst.shape),
            resident(ag.shape),
            resident(gg.shape),
            resident(w_out.shape),
        ],
        out_specs=pl.BlockSpec((tb, d), lambda i: (second_stage_tile(i), 0)),
        scratch_shapes=[
            pltpu.VMEM((tb, d), BF16),
            pltpu.VMEM((tb, w_in.shape[1]), F32),
            pltpu.VMEM((tb, ATTN_WIDTH), BF16),
            pltpu.VMEM((2 * N_KV_HEADS, BLOCK + tb, LANES), BF16),
            pltpu.VMEM((2 * N_KV_HEADS, BLOCK + tb, LANES), BF16),
            pltpu.VMEM((tb, ATTN_WIDTH), F32),
            pltpu.VMEM((tb, GMLP_WIDTH), F32),
            pltpu.VMEM((tb, ATTN_WIDTH + GMLP_WIDTH), BF16),
            pltpu.VMEM((tb, d), F32),
        ],
        compiler_params=pltpu.CompilerParams(
            dimension_semantics=("arbitrary",),
            vmem_limit_bytes=V7X_VMEM_LIMIT_BYTES,
        ),
        name="mixer",
    )(x, pos, invf, gm, w_in, qg, kg, sinks, gvg, w_s, bst, ag, gg, w_out)


def kernel(x, positions, ffn1_norm, ffn1_w_gate, ffn1_w_up, ffn1_w_down, mix_norm, w_in,
           q_norm, k_norm, attn_sinks, gmlp_v_norm, gmlp_w_s, gmlp_b_s, attn_out_norm,
           gmlp_out_norm, w_out, ffn2_norm, ffn2_w_gate, ffn2_w_up, ffn2_w_down):
    b, s, d = x.shape
    depth = w_in.shape[0]
    xt = x.reshape(b * s, d)
    pos = positions.reshape(b * s // MIXER_TOKEN_TILE, 1, MIXER_TOKEN_TILE)
    invf = (ROPE_THETA ** (-jnp.arange(0, ROT_DIM, 2, dtype=F32) / ROT_DIM)).reshape(ROT_DIM // 2, 1)

    for l in range(depth):
        xt = _ffn(xt, ffn1_norm[l].reshape(1, d), ffn1_w_gate[l].astype(BF16),
                  ffn1_w_up[l].astype(BF16), ffn1_w_down[l].astype(BF16))
        xt = _mixer(
            xt, pos, invf, mix_norm[l].reshape(1, d), w_in[l].astype(BF16),
            (jnp.tile(q_norm[l], 2) * (HEAD_DIM ** -0.5)).reshape(1, LANES),
            jnp.tile(k_norm[l], 2).reshape(1, LANES),
            attn_sinks[l], gmlp_v_norm[l].reshape(1, GMLP_WIDTH), gmlp_w_s[l], gmlp_b_s[l].T,
            attn_out_norm[l].reshape(1, ATTN_WIDTH), gmlp_out_norm[l].reshape(1, GMLP_WIDTH),
            w_out[l].astype(BF16), seq_len=s)
        xt = _ffn(xt, ffn2_norm[l].reshape(1, d), ffn2_w_gate[l].astype(BF16),
                  ffn2_w_up[l].astype(BF16), ffn2_w_down[l].astype(BF16))
    return xt.reshape(b, s, d)
```

```python
import functools

import jax
import jax.numpy as jnp
from jax import lax
from jax.experimental import pallas as pl
from jax.experimental.pallas import tpu as pltpu

HEAD_DIM = 64
N_Q_HEADS = 16
N_KV_HEADS = 4
ATTN_WIDTH = N_Q_HEADS * HEAD_DIM
KV_WIDTH = N_KV_HEADS * HEAD_DIM
BLOCK = 128
ROT_DIM = HEAD_DIM // 4
ROPE_THETA = 500000.0
GMLP_GROUP_DIM = 128
GMLP_WIDTH = 1024
N_GMLP_GROUPS = GMLP_WIDTH // GMLP_GROUP_DIM
EPS = 1e-6
NEG_INF = -1e30

LANES = 128
V7X_VMEM_LIMIT_BYTES = 60000 * 1024

FFN_TOKEN_TILE = 1024
FFN_FF_TILE = 512
FFN_NORM_CHUNKS = 4
MIXER_TOKEN_TILE = 256

F32 = jnp.float32
BF16 = jnp.bfloat16


def _rmsnorm_rows(x, gain):
    y = x * lax.rsqrt(jnp.mean(x * x, axis=-1, keepdims=True) + EPS)
    return y * gain


def _ffn_kernel(x_ref, gain_ref, wg_ref, wu_ref, wd_ref, o_ref, h_ref):
    f = pl.program_id(1)

    def activate(g, u):
        return (jax.nn.silu(g) * (0.5 * u)).astype(BF16)

    @pl.when(f == 0)
    def _():
        rows = h_ref.shape[0] // FFN_NORM_CHUNKS
        hs, gus = [], []

        def norm(r):
            sl = slice(r * rows, (r + 1) * rows)
            h = _rmsnorm_rows(x_ref[sl, :], gain_ref[...]).astype(BF16)
            h_ref[sl, :] = h
            return h

        def gate_up(h):
            return (jnp.dot(h, wg_ref[...], preferred_element_type=F32),
                    jnp.dot(h, wu_ref[...], preferred_element_type=F32))

        def down(r, a):
            sl = slice(r * rows, (r + 1) * rows)
            o_ref[sl, :] = x_ref[sl, :] + jnp.dot(a, wd_ref[...], preferred_element_type=F32)

        hs.append(norm(0))
        gus.append(gate_up(hs[0]))
        for r in range(FFN_NORM_CHUNKS):
            if r + 1 < FFN_NORM_CHUNKS:
                hs.append(norm(r + 1))
                gus.append(gate_up(hs[r + 1]))
            down(r, activate(*gus[r]))

    @pl.when(f > 0)
    def _():
        h = h_ref[...]
        g = jnp.dot(h, wg_ref[...], preferred_element_type=F32)
        u = jnp.dot(h, wu_ref[...], preferred_element_type=F32)
        o_ref[...] += jnp.dot(activate(g, u), wd_ref[...], preferred_element_type=F32)


def _ffn(x, gain, w_gate, w_up, w_down):
    t, d = x.shape
    d_ff = w_gate.shape[1]
    tm, tf = FFN_TOKEN_TILE, FFN_FF_TILE
    assert t % tm == 0 and d_ff % tf == 0
    return pl.pallas_call(
        _ffn_kernel,
        out_shape=jax.ShapeDtypeStruct((t, d), F32),
        grid=(t // tm, d_ff // tf),
        in_specs=[
            pl.BlockSpec((tm, d), lambda i, f: (i, 0)),
            pl.BlockSpec((1, d), lambda i, f: (0, 0)),
            pl.BlockSpec((d, tf), lambda i, f: (0, f)),
            pl.BlockSpec((d, tf), lambda i, f: (0, f)),
            pl.BlockSpec((tf, d), lambda i, f: (f, 0)),
        ],
        out_specs=pl.BlockSpec((tm, d), lambda i, f: (i, 0)),
        scratch_shapes=[pltpu.VMEM((tm, d), BF16)],
        compiler_params=pltpu.CompilerParams(
            dimension_semantics=("arbitrary", "arbitrary"),
            vmem_limit_bytes=V7X_VMEM_LIMIT_BYTES,
        ),
        name="ffn",
    )(x, gain, w_gate, w_up, w_down)


def _block_diag_ones(size, group):
    r = lax.broadcasted_iota(jnp.int32, (size, size), 0) // group
    c = lax.broadcasted_iota(jnp.int32, (size, size), 1) // group
    return jnp.where(r == c, 1.0, 0.0).astype(BF16)


def _mixer_kernel(xa_ref, pos_ref, invf_ref, gm_ref, win_ref, qg_ref, kg_ref, sinks_ref,
                  gvg_ref, ws_ref, bst_ref, ag_ref, gg_ref, wout_ref, o_ref,
                  h_ref, z_ref, q_ref, kpad_ref, vpad_ref, a_ref, g_ref, m_ref, x_ref,
                  *, tb, tiles_per_seq):
    i = pl.program_id(0)
    n_blk = tb // BLOCK
    chunk = 2 * LANES
    half = ROT_DIM // 2
    col_k = ATTN_WIDTH
    col_v = col_k + KV_WIDTH
    col_gu = col_v + KV_WIDTH
    col_gv = col_gu + GMLP_WIDTH

    @pl.when(i == 0)
    def _():
        z_ref[...] = jnp.zeros_like(z_ref)
        x_ref[...] = jnp.zeros_like(x_ref)
        kpad_ref[:, :BLOCK, :] = jnp.zeros((2 * N_KV_HEADS, BLOCK, LANES), BF16)
        vpad_ref[:, :BLOCK, :] = jnp.zeros((2 * N_KV_HEADS, BLOCK, LANES), BF16)

    lane = lax.broadcasted_iota(jnp.int32, (1, LANES), 1)
    lo_half = lane < HEAD_DIM

    ang = invf_ref[...] * pos_ref[jnp.maximum(i - 1, 0)].astype(F32)
    cos, sin = jnp.cos(ang), jnp.sin(ang)
    rest_one = jnp.ones((HEAD_DIM - ROT_DIM, tb), F32)
    rest_zero = jnp.zeros((HEAD_DIM - ROT_DIM, tb), F32)
    zero = jnp.zeros((half, tb), F32)
    rope_c = jnp.concatenate([cos, cos, rest_one] * 2, axis=0).T
    rope_s1 = jnp.concatenate([-sin, zero, rest_zero] * 2, axis=0).T
    rope_s2 = jnp.concatenate([zero, sin, rest_zero] * 2, axis=0).T

    bd_head = _block_diag_ones(chunk, HEAD_DIM)
    bd_group = _block_diag_ones(chunk, GMLP_GROUP_DIM)

    def rope(t):
        return t * rope_c + pltpu.roll(t, LANES - half, 1) * rope_s1 + pltpu.roll(t, half, 1) * rope_s2

    def x_norm():
        h_ref[...] = _rmsnorm_rows(xa_ref[...], gm_ref[...]).astype(BF16)

    def proj(c0):
        cols = slice(c0, c0 + chunk)
        z_ref[:, cols] = jnp.dot(h_ref[...], win_ref[:, cols], preferred_element_type=F32)

    def heads_load(c0):
        z = z_ref[:, c0:c0 + chunk]
        return z, (z * z).astype(BF16)

    def heads_dot(sq):
        return jnp.dot(sq, bd_head, preferred_element_type=F32)

    def heads_finish(z, ss, gain):
        zn = z * lax.rsqrt(ss * (1.0 / HEAD_DIM) + EPS)
        return [rope(zn[:, j * LANES:(j + 1) * LANES] * gain) for j in range(2)]

    def q_finish(c, z, ss):
        s0, s1 = heads_finish(z, ss, qg_ref[...])
        q_ref[:, c * chunk:(c + 1) * chunk] = jnp.concatenate([s0, s1], axis=1).astype(BF16)

    def place_heads(slab, dst_ref, j):
        rolled = pltpu.roll(slab, HEAD_DIM, 1)
        zero_slab = jnp.zeros_like(slab)
        cur = slice(BLOCK, BLOCK + tb)
        dst_ref[4 * j + 0, cur, :] = jnp.where(lo_half, slab, zero_slab).astype(BF16)
        dst_ref[4 * j + 1, cur, :] = jnp.where(lo_half, zero_slab, rolled).astype(BF16)
        dst_ref[4 * j + 2, cur, :] = jnp.where(lo_half, rolled, zero_slab).astype(BF16)
        dst_ref[4 * j + 3, cur, :] = jnp.where(lo_half, zero_slab, slab).astype(BF16)

    def kv_finish(zk, ss, zv):
        k_slabs = heads_finish(zk, ss, kg_ref[...])
        for j in range(2):
            place_heads(k_slabs[j], kpad_ref, j)
            place_heads(zv[:, j * LANES:(j + 1) * LANES], vpad_ref, j)

    srow = lax.broadcasted_iota(jnp.int32, (2 * BLOCK, 4 * BLOCK), 0) % BLOCK
    scol = lax.broadcasted_iota(jnp.int32, (2 * BLOCK, 4 * BLOCK), 1) % (2 * BLOCK)
    row_first_slab = lax.broadcasted_iota(jnp.int32, (2 * BLOCK, 1), 0) < BLOCK
    seq_start = ((i + tiles_per_seq - 1) % tiles_per_seq) == 0
    ones_lo = jnp.broadcast_to(jnp.where(lo_half, 1.0, 0.0).astype(BF16), (2 * BLOCK, LANES))
    ones_hi = jnp.broadcast_to(jnp.where(lo_half, 0.0, 1.0).astype(BF16), (2 * BLOCK, LANES))

    def band_mask(b):
        prev_off = jnp.where(seq_start, BLOCK, 0) if b == 0 else 0
        return (((scol < BLOCK) & (scol > srow + prev_off))
                | ((scol >= BLOCK) & (scol - BLOCK <= srow)))

    def att_scores(b, hkv):
        qrows = slice(b * BLOCK, (b + 1) * BLOCK)
        krows = slice(b * BLOCK, (b + 2) * BLOCK)
        q2 = jnp.concatenate([q_ref[qrows, (2 * hkv) * LANES:(2 * hkv + 1) * LANES],
                              q_ref[qrows, (2 * hkv + 1) * LANES:(2 * hkv + 2) * LANES]], axis=0)
        kcat = jnp.concatenate([kpad_ref[2 * hkv, krows, :], kpad_ref[2 * hkv + 1, krows, :]], axis=0)
        return lax.dot_general(q2, kcat, (((1,), (1,)), ((), ())), preferred_element_type=F32)

    def att_softmax(b, hkv, s):
        s = jnp.where(band_mask(b), s, NEG_INF)
        sink_lo = jnp.where(row_first_slab, sinks_ref[4 * hkv], sinks_ref[4 * hkv + 2])
        sink_hi = jnp.where(row_first_slab, sinks_ref[4 * hkv + 1], sinks_ref[4 * hkv + 3])
        s_lo, s_hi = s[:, :2 * BLOCK], s[:, 2 * BLOCK:]
        m_lo = jnp.maximum(jnp.max(s_lo, axis=1, keepdims=True), sink_lo)
        m_hi = jnp.maximum(jnp.max(s_hi, axis=1, keepdims=True), sink_hi)
        p = jnp.concatenate([jnp.exp(s_lo - m_lo), jnp.exp(s_hi - m_hi)], axis=1).astype(BF16)
        sink_mass = jnp.where(lo_half, jnp.exp(sink_lo - m_lo), jnp.exp(sink_hi - m_hi))
        return p, sink_mass

    def att_values(b, hkv, p):
        krows = slice(b * BLOCK, (b + 2) * BLOCK)
        vcat = jnp.concatenate(
            [jnp.concatenate([vpad_ref[2 * hkv, krows, :], ones_lo], axis=1),
             jnp.concatenate([vpad_ref[2 * hkv + 1, krows, :], ones_hi], axis=1)], axis=0)
        return jnp.dot(p, vcat, preferred_element_type=F32)

    def att_finish(b, hkv, o2, sink_mass):
        qrows = slice(b * BLOCK, (b + 1) * BLOCK)
        out = o2[:, :LANES] / (o2[:, LANES:] + sink_mass)
        a_ref[qrows, (2 * hkv) * LANES:(2 * hkv + 1) * LANES] = out[:BLOCK]
        a_ref[qrows, (2 * hkv + 1) * LANES:(2 * hkv + 2) * LANES] = out[BLOCK:]

    def carry_band():
        last = slice(tb, tb + BLOCK)
        kpad_ref[:, :BLOCK, :] = kpad_ref[:, last, :]
        vpad_ref[:, :BLOCK, :] = vpad_ref[:, last, :]

    trow = lax.broadcasted_iota(jnp.int32, (BLOCK, BLOCK), 0)
    tcol = lax.broadcasted_iota(jnp.int32, (BLOCK, BLOCK), 1)
    causal = tcol <= trow

    def gmlp_act(c):
        gu = jax.nn.gelu(z_ref[:, col_gu + c * chunk:col_gu + (c + 1) * chunk])
        gv = jax.nn.gelu(z_ref[:, col_gv + c * chunk:col_gv + (c + 1) * chunk])
        return gu, gv, (gv * gv).astype(BF16)

    def gmlp_norm_dot(sq):
        return jnp.dot(sq, bd_group, preferred_element_type=F32)

    def gmlp_scale(c, gv, ss):
        gv = gv * lax.rsqrt(ss * (1.0 / GMLP_GROUP_DIM) + EPS)
        return (gv * gvg_ref[:, c * chunk:(c + 1) * chunk]).astype(BF16)

    def gmlp_spatial_dots(c, gvn):
        sps = []
        for jj in range(2):
            w = jnp.where(causal, ws_ref[2 * c + jj], 0.0).astype(BF16)
            for b in range(n_blk):
                rws = slice(b * BLOCK, (b + 1) * BLOCK)
                sps.append(jnp.dot(w, gvn[rws, jj * LANES:(jj + 1) * LANES], preferred_element_type=F32))
        return sps

    def gmlp_finish(c, gu, sps):
        for jj in range(2):
            grp = 2 * c + jj
            bias = bst_ref[:, grp:grp + 1]
            for b in range(n_blk):
                rws = slice(b * BLOCK, (b + 1) * BLOCK)
                g_ref[rws, grp * LANES:(grp + 1) * LANES] = (
                    gu[rws, jj * LANES:(jj + 1) * LANES] * (sps[jj * n_blk + b] + bias))

    def a_norm():
        m_ref[:, :ATTN_WIDTH] = _rmsnorm_rows(a_ref[...], ag_ref[...]).astype(BF16)

    def g_norm():
        m_ref[:, ATTN_WIDTH:] = _rmsnorm_rows(g_ref[...], gg_ref[...]).astype(BF16)

    def out_a(n):
        cols = slice(n * chunk, (n + 1) * chunk)
        o_ref[:, cols] = x_ref[:, cols] + jnp.dot(m_ref[:, :ATTN_WIDTH], wout_ref[:ATTN_WIDTH, cols],
                                                  preferred_element_type=F32)
        x_ref[:, cols] = xa_ref[:, cols]

    def out_g(n):
        cols = slice(n * chunk, (n + 1) * chunk)
        o_ref[:, cols] += jnp.dot(m_ref[:, ATTN_WIDTH:], wout_ref[ATTN_WIDTH:, cols],
                                  preferred_element_type=F32)

    n_q = ATTN_WIDTH // chunk
    n_g = GMLP_WIDTH // chunk
    n_out = o_ref.shape[1] // chunk
    units = [(b, hkv) for b in range(n_blk) for hkv in range(N_KV_HEADS)]
    assert len(units) == 2 * n_g

    x_norm()
    q_in = [heads_load(c * chunk) for c in range(n_q)]
    zk, k_sq = heads_load(col_k)
    zv = z_ref[:, col_v:col_v + KV_WIDTH]
    proj(0)
    q_ss = [heads_dot(sq) for _, sq in q_in]
    k_ss = heads_dot(k_sq)
    for c in range(n_q):
        q_finish(c, q_in[c][0], q_ss[c])
        if c % 2 == 1:
            proj((c + 1) // 2 * chunk)
    kv_finish(zk, k_ss, zv)
    proj((n_q - 1) * chunk)

    big = [col_k, col_v]
    for c in range(n_g):
        big += [col_gu + c * chunk, col_gv + c * chunk]
    scores, probs, act, scaled, spatial = {}, {}, {}, {}, {}
    for r in range(len(units) + 1):
        c = r // 2
        if r >= 1:
            weighted = att_values(*units[r - 1], probs[r - 1][0])
        if r < len(units):
            scores[r] = att_scores(*units[r])
        if r % 2 == 1:
            norm_ss = gmlp_norm_dot(act[c][2])
        elif c >= 1:
            spatial[c - 1] = gmlp_spatial_dots(c - 1, scaled[c - 1])
        if r >= 1:
            att_finish(*units[r - 1], weighted, probs[r - 1][1])
        if r < len(units):
            probs[r] = att_softmax(*units[r], scores[r])
        if r % 2 == 1:
            scaled[c] = gmlp_scale(c, act[c][1], norm_ss)
        else:
            if c >= 1:
                gmlp_finish(c - 1, act[c - 1][0], spatial[c - 1])
            if c < n_g:
                act[c] = gmlp_act(c)
        proj(big[r])
    carry_band()
    a_norm()
    proj(big[-1])
    for n in range(n_out):
        out_a(n)
    g_norm()
    for n in range(n_out):
        out_g(n)


def _mixer(x, pos, invf, gm, w_in, qg, kg, sinks, gvg, w_s, bst, ag, gg, w_out, *, seq_len):
    t, d = x.shape
    tb = MIXER_TOKEN_TILE
    assert t % tb == 0 and seq_len % tb == 0 and tb % BLOCK == 0
    n_tiles = t // tb

    def resident(shape):
        return pl.BlockSpec(shape, lambda i: (0,) * len(shape), pipeline_mode=pl.Buffered(1))

    def first_stage_tile(i):
        return jnp.minimum(i, n_tiles - 1)

    def second_stage_tile(i):
        return jnp.maximum(i - 1, 0)

    return pl.pallas_call(
        functools.partial(_mixer_kernel, tb=tb, tiles_per_seq=seq_len // tb),
        out_shape=jax.ShapeDtypeStruct((t, d), F32),
        grid=(n_tiles + 1,),
        in_specs=[
            pl.BlockSpec((tb, d), lambda i: (first_stage_tile(i), 0)),
            resident(pos.shape),
            resident(invf.shape),
            resident(gm.shape),
            resident(w_in.shape),
            resident(qg.shape),
            resident(kg.shape),
            pl.BlockSpec(memory_space=pltpu.SMEM),
            resident(gvg.shape),
            resident(w_s.shape),
            resident(bst.shape),
            resident(ag.shape),
            resident(gg.shape),
            resident(w_out.shape),
        ],
        out_specs=pl.BlockSpec((tb, d), lambda i: (second_stage_tile(i), 0)),
        scratch_shapes=[
            pltpu.VMEM((tb, d), BF16),
            pltpu.VMEM((tb, w_in.shape[1]), F32),
            pltpu.VMEM((tb, ATTN_WIDTH), BF16),
            pltpu.VMEM((2 * N_KV_HEADS, BLOCK + tb, LANES), BF16),
            pltpu.VMEM((2 * N_KV_HEADS, BLOCK + tb, LANES), BF16),
            pltpu.VMEM((tb, ATTN_WIDTH), F32),
            pltpu.VMEM((tb, GMLP_WIDTH), F32),
            pltpu.VMEM((tb, ATTN_WIDTH + GMLP_WIDTH), BF16),
            pltpu.VMEM((tb, d), F32),
        ],
        compiler_params=pltpu.CompilerParams(
            dimension_semantics=("arbitrary",),
            vmem_limit_bytes=V7X_VMEM_LIMIT_BYTES,
        ),
        name="mixer",
    )(x, pos, invf, gm, w_in, qg, kg, sinks, gvg, w_s, bst, ag, gg, w_out)


def kernel(x, positions, ffn1_norm, ffn1_w_gate, ffn1_w_up, ffn1_w_down, mix_norm, w_in,
           q_norm, k_norm, attn_sinks, gmlp_v_norm, gmlp_w_s, gmlp_b_s, attn_out_norm,
           gmlp_out_norm, w_out, ffn2_norm, ffn2_w_gate, ffn2_w_up, ffn2_w_down):
    b, s, d = x.shape
    depth = w_in.shape[0]
    xt = x.reshape(b * s, d)
    pos = positions.reshape(b * s // MIXER_TOKEN_TILE, 1, MIXER_TOKEN_TILE)
    invf = (ROPE_THETA ** (-jnp.arange(0, ROT_DIM, 2, dtype=F32) / ROT_DIM)).reshape(ROT_DIM // 2, 1)

    for l in range(depth):
        xt = _ffn(xt, ffn1_norm[l].reshape(1, d), ffn1_w_gate[l].astype(BF16),
                  ffn1_w_up[l].astype(BF16), ffn1_w_down[l].astype(BF16))
        xt = _mixer(
            xt, pos, invf, mix_norm[l].reshape(1, d), w_in[l].astype(BF16),
            (jnp.tile(q_norm[l], 2) * (HEAD_DIM ** -0.5)).reshape(1, LANES),
            jnp.tile(k_norm[l], 2).reshape(1, LANES),
            attn_sinks[l], gmlp_v_norm[l].reshape(1, GMLP_WIDTH), gmlp_w_s[l], gmlp_b_s[l].T,
            attn_out_norm[l].reshape(1, ATTN_WIDTH), gmlp_out_norm[l].reshape(1, GMLP_WIDTH),
            w_out[l].astype(BF16), seq_len=s)
        xt = _ffn(xt, ffn2_norm[l].reshape(1, d), ffn2_w_gate[l].astype(BF16),
                  ffn2_w_up[l].astype(BF16), ffn2_w_down[l].astype(BF16))
    return xt.reshape(b, s, d)
```

```python
import functools

import jax
import jax.numpy as jnp
from jax import lax
from jax.experimental import pallas as pl
from jax.experimental.pallas import tpu as pltpu

HEAD_DIM = 64
N_Q_HEADS = 16
N_KV_HEADS = 4
ATTN_WIDTH = N_Q_HEADS * HEAD_DIM
KV_WIDTH = N_KV_HEADS * HEAD_DIM
BLOCK = 128
ROT_DIM = HEAD_DIM // 4
ROPE_THETA = 500000.0
GMLP_GROUP_DIM = 128
GMLP_WIDTH = 1024
N_GMLP_GROUPS = GMLP_WIDTH // GMLP_GROUP_DIM
EPS = 1e-6
NEG_INF = -1e30

LANES = 128
BF16_SUBLANES = 16
V7X_VMEM_LIMIT_BYTES = 60000 * 1024

FFN_TOKEN_TILE = 1024
FFN_FF_TILE = 512
FFN_NORM_CHUNKS = 4
CAST_BLOCK_COLS = 512
MIXER_TOKEN_TILE = 256

F32 = jnp.float32
BF16 = jnp.bfloat16


def _rmsnorm_rows(x, gain):
    y = x * lax.rsqrt(jnp.mean(x * x, axis=-1, keepdims=True) + EPS)
    return y * gain


def _ffn_kernel(*refs, n_casts):
    x_ref, gain_ref, wg_ref, wu_ref, wd_ref = refs[:5]
    cast_in = refs[5:5 + n_casts]
    o_ref = refs[5 + n_casts]
    cast_out = refs[6 + n_casts:6 + 2 * n_casts]
    h_ref = refs[6 + 2 * n_casts]
    f = pl.program_id(1)

    def activate(g, u):
        return (jax.nn.silu(g) * (0.5 * u)).astype(BF16)

    def side_casts():
        for src, dst in zip(cast_in, cast_out):
            dst[...] = src[...].astype(BF16)

    @pl.when(f == 0)
    def _():
        rows = h_ref.shape[0] // FFN_NORM_CHUNKS
        hs, gus = [], []

        def norm(r):
            sl = slice(r * rows, (r + 1) * rows)
            h = _rmsnorm_rows(x_ref[sl, :], gain_ref[...]).astype(BF16)
            h_ref[sl, :] = h
            return h

        def gate_up(h):
            return (jnp.dot(h, wg_ref[...], preferred_element_type=F32),
                    jnp.dot(h, wu_ref[...], preferred_element_type=F32))

        def down(r, a):
            sl = slice(r * rows, (r + 1) * rows)
            o_ref[sl, :] = x_ref[sl, :] + jnp.dot(a, wd_ref[...], preferred_element_type=F32)

        hs.append(norm(0))
        gus.append(gate_up(hs[0]))
        for r in range(FFN_NORM_CHUNKS):
            if r + 1 < FFN_NORM_CHUNKS:
                hs.append(norm(r + 1))
                gus.append(gate_up(hs[r + 1]))
            down(r, activate(*gus[r]))
        side_casts()

    @pl.when(f > 0)
    def _():
        h = h_ref[...]
        g = jnp.dot(h, wg_ref[...], preferred_element_type=F32)
        u = jnp.dot(h, wu_ref[...], preferred_element_type=F32)
        o_ref[...] += jnp.dot(activate(g, u), wd_ref[...], preferred_element_type=F32)
        side_casts()


def _ffn(x, gain, w_gate, w_up, w_down, casts=()):
    t, d = x.shape
    d_ff = w_gate.shape[1]
    tm, tf = FFN_TOKEN_TILE, FFN_FF_TILE
    assert t % tm == 0 and d_ff % tf == 0
    n_tiles, n_ff = t // tm, d_ff // tf

    cast_specs = []
    for w in casts:
        rows, cols = w.shape
        assert rows % (n_tiles * BF16_SUBLANES) == 0 and cols % CAST_BLOCK_COLS == 0
        last = cols // CAST_BLOCK_COLS - 1
        assert last < n_ff
        cast_specs.append(pl.BlockSpec((rows // n_tiles, CAST_BLOCK_COLS),
                                       lambda i, f, last=last: (i, jnp.minimum(f, last))))

    outs = pl.pallas_call(
        functools.partial(_ffn_kernel, n_casts=len(casts)),
        out_shape=[jax.ShapeDtypeStruct((t, d), F32)] + [jax.ShapeDtypeStruct(w.shape, BF16) for w in casts],
        grid=(n_tiles, n_ff),
        in_specs=[
            pl.BlockSpec((tm, d), lambda i, f: (i, 0)),
            pl.BlockSpec((1, d), lambda i, f: (0, 0)),
            pl.BlockSpec((d, tf), lambda i, f: (0, f)),
            pl.BlockSpec((d, tf), lambda i, f: (0, f)),
            pl.BlockSpec((tf, d), lambda i, f: (f, 0)),
        ] + cast_specs,
        out_specs=[pl.BlockSpec((tm, d), lambda i, f: (i, 0))] + cast_specs,
        scratch_shapes=[pltpu.VMEM((tm, d), BF16)],
        compiler_params=pltpu.CompilerParams(
            dimension_semantics=("arbitrary", "arbitrary"),
            vmem_limit_bytes=V7X_VMEM_LIMIT_BYTES,
        ),
        name="ffn",
    )(x, gain, w_gate, w_up, w_down, *casts)
    return outs[0], list(outs[1:])


def _block_diag_ones(size, group):
    r = lax.broadcasted_iota(jnp.int32, (size, size), 0) // group
    c = lax.broadcasted_iota(jnp.int32, (size, size), 1) // group
    return jnp.where(r == c, 1.0, 0.0).astype(BF16)


def _mixer_kernel(xa_ref, pos_ref, invf_ref, gm_ref, win_ref, qg_ref, kg_ref, sinks_ref,
                  gvg_ref, ws_ref, bst_ref, ag_ref, gg_ref, wout_ref, o_ref,
                  h_ref, z_ref, q_ref, kpad_ref, vpad_ref, a_ref, g_ref, m_ref, x_ref,
                  *, tb, tiles_per_seq):
    i = pl.program_id(0)
    n_blk = tb // BLOCK
    chunk = 2 * LANES
    half = ROT_DIM // 2
    col_k = ATTN_WIDTH
    col_v = col_k + KV_WIDTH
    col_gu = col_v + KV_WIDTH
    col_gv = col_gu + GMLP_WIDTH

    @pl.when(i == 0)
    def _():
        z_ref[...] = jnp.zeros_like(z_ref)
        x_ref[...] = jnp.zeros_like(x_ref)
        kpad_ref[:, :BLOCK, :] = jnp.zeros((2 * N_KV_HEADS, BLOCK, LANES), BF16)
        vpad_ref[:, :BLOCK, :] = jnp.zeros((2 * N_KV_HEADS, BLOCK, LANES), BF16)

    lane = lax.broadcasted_iota(jnp.int32, (1, LANES), 1)
    lo_half = lane < HEAD_DIM

    ang = invf_ref[...] * pos_ref[jnp.maximum(i - 1, 0)].astype(F32)
    cos, sin = jnp.cos(ang), jnp.sin(ang)
    rest_one = jnp.ones((HEAD_DIM - ROT_DIM, tb), F32)
    rest_zero = jnp.zeros((HEAD_DIM - ROT_DIM, tb), F32)
    zero = jnp.zeros((half, tb), F32)
    rope_c = jnp.concatenate([cos, cos, rest_one] * 2, axis=0).T
    rope_s1 = jnp.concatenate([-sin, zero, rest_zero] * 2, axis=0).T
    rope_s2 = jnp.concatenate([zero, sin, rest_zero] * 2, axis=0).T

    bd_head = _block_diag_ones(chunk, HEAD_DIM)
    bd_group = _block_diag_ones(chunk, GMLP_GROUP_DIM)

    def rope(t):
        return t * rope_c + pltpu.roll(t, LANES - half, 1) * rope_s1 + pltpu.roll(t, half, 1) * rope_s2

    def x_norm():
        h_ref[...] = _rmsnorm_rows(xa_ref[...], gm_ref[...]).astype(BF16)

    def proj(c0):
        cols = slice(c0, c0 + chunk)
        z_ref[:, cols] = jnp.dot(h_ref[...], win_ref[:, cols], preferred_element_type=F32)

    def heads_load(c0):
        z = z_ref[:, c0:c0 + chunk]
        return z, (z * z).astype(BF16)

    def heads_dot(sq):
        return jnp.dot(sq, bd_head, preferred_element_type=F32)

    def heads_finish(z, ss, gain):
        zn = z * lax.rsqrt(ss * (1.0 / HEAD_DIM) + EPS)
        return [rope(zn[:, j * LANES:(j + 1) * LANES] * gain) for j in range(2)]

    def q_finish(c, z, ss):
        s0, s1 = heads_finish(z, ss, qg_ref[...])
        q_ref[:, c * chunk:(c + 1) * chunk] = jnp.concatenate([s0, s1], axis=1).astype(BF16)

    def place_heads(slab, dst_ref, j):
        rolled = pltpu.roll(slab, HEAD_DIM, 1)
        zero_slab = jnp.zeros_like(slab)
        cur = slice(BLOCK, BLOCK + tb)
        dst_ref[4 * j + 0, cur, :] = jnp.where(lo_half, slab, zero_slab).astype(BF16)
        dst_ref[4 * j + 1, cur, :] = jnp.where(lo_half, zero_slab, rolled).astype(BF16)
        dst_ref[4 * j + 2, cur, :] = jnp.where(lo_half, rolled, zero_slab).astype(BF16)
        dst_ref[4 * j + 3, cur, :] = jnp.where(lo_half, zero_slab, slab).astype(BF16)

    def kv_finish(zk, ss, zv):
        k_slabs = heads_finish(zk, ss, kg_ref[...])
        for j in range(2):
            place_heads(k_slabs[j], kpad_ref, j)
            place_heads(zv[:, j * LANES:(j + 1) * LANES], vpad_ref, j)

    srow = lax.broadcasted_iota(jnp.int32, (2 * BLOCK, 4 * BLOCK), 0) % BLOCK
    scol = lax.broadcasted_iota(jnp.int32, (2 * BLOCK, 4 * BLOCK), 1) % (2 * BLOCK)
    row_first_slab = lax.broadcasted_iota(jnp.int32, (2 * BLOCK, 1), 0) < BLOCK
    seq_start = ((i + tiles_per_seq - 1) % tiles_per_seq) == 0
    ones_lo = jnp.broadcast_to(jnp.where(lo_half, 1.0, 0.0).astype(BF16), (2 * BLOCK, LANES))
    ones_hi = jnp.broadcast_to(jnp.where(lo_half, 0.0, 1.0).astype(BF16), (2 * BLOCK, LANES))

    def band_mask(b):
        prev_off = jnp.where(seq_start, BLOCK, 0) if b == 0 else 0
        return (((scol < BLOCK) & (scol > srow + prev_off))
                | ((scol >= BLOCK) & (scol - BLOCK <= srow)))

    def att_scores(b, hkv):
        qrows = slice(b * BLOCK, (b + 1) * BLOCK)
        krows = slice(b * BLOCK, (b + 2) * BLOCK)
        q2 = jnp.concatenate([q_ref[qrows, (2 * hkv) * LANES:(2 * hkv + 1) * LANES],
                              q_ref[qrows, (2 * hkv + 1) * LANES:(2 * hkv + 2) * LANES]], axis=0)
        kcat = jnp.concatenate([kpad_ref[2 * hkv, krows, :], kpad_ref[2 * hkv + 1, krows, :]], axis=0)
        return lax.dot_general(q2, kcat, (((1,), (1,)), ((), ())), preferred_element_type=F32)

    def att_softmax(b, hkv, s):
        s = jnp.where(band_mask(b), s, NEG_INF)
        sink_lo = jnp.where(row_first_slab, sinks_ref[4 * hkv], sinks_ref[4 * hkv + 2])
        sink_hi = jnp.where(row_first_slab, sinks_ref[4 * hkv + 1], sinks_ref[4 * hkv + 3])
        s_lo, s_hi = s[:, :2 * BLOCK], s[:, 2 * BLOCK:]
        m_lo = jnp.maximum(jnp.max(s_lo, axis=1, keepdims=True), sink_lo)
        m_hi = jnp.maximum(jnp.max(s_hi, axis=1, keepdims=True), sink_hi)
        p = jnp.concatenate([jnp.exp(s_lo - m_lo), jnp.exp(s_hi - m_hi)], axis=1).astype(BF16)
        sink_mass = jnp.where(lo_half, jnp.exp(sink_lo - m_lo), jnp.exp(sink_hi - m_hi))
        return p, sink_mass

    def att_values(b, hkv, p):
        krows = slice(b * BLOCK, (b + 2) * BLOCK)
        vcat = jnp.concatenate(
            [jnp.concatenate([vpad_ref[2 * hkv, krows, :], ones_lo], axis=1),
             jnp.concatenate([vpad_ref[2 * hkv + 1, krows, :], ones_hi], axis=1)], axis=0)
        return jnp.dot(p, vcat, preferred_element_type=F32)

    def att_finish(b, hkv, o2, sink_mass):
        qrows = slice(b * BLOCK, (b + 1) * BLOCK)
        out = o2[:, :LANES] / (o2[:, LANES:] + sink_mass)
        a_ref[qrows, (2 * hkv) * LANES:(2 * hkv + 1) * LANES] = out[:BLOCK]
        a_ref[qrows, (2 * hkv + 1) * LANES:(2 * hkv + 2) * LANES] = out[BLOCK:]

    def carry_band():
        last = slice(tb, tb + BLOCK)
        kpad_ref[:, :BLOCK, :] = kpad_ref[:, last, :]
        vpad_ref[:, :BLOCK, :] = vpad_ref[:, last, :]

    trow = lax.broadcasted_iota(jnp.int32, (BLOCK, BLOCK), 0)
    tcol = lax.broadcasted_iota(jnp.int32, (BLOCK, BLOCK), 1)
    causal = tcol <= trow

    def gmlp_act(c):
        gu = jax.nn.gelu(z_ref[:, col_gu + c * chunk:col_gu + (c + 1) * chunk])
        gv = jax.nn.gelu(z_ref[:, col_gv + c * chunk:col_gv + (c + 1) * chunk])
        return gu, gv, (gv * gv).astype(BF16)

    def gmlp_norm_dot(sq):
        return jnp.dot(sq, bd_group, preferred_element_type=F32)

    def gmlp_scale(c, gv, ss):
        gv = gv * lax.rsqrt(ss * (1.0 / GMLP_GROUP_DIM) + EPS)
        return (gv * gvg_ref[:, c * chunk:(c + 1) * chunk]).astype(BF16)

    def gmlp_spatial_dots(c, gvn):
        sps = []
        for jj in range(2):
            w = jnp.where(causal, ws_ref[2 * c + jj], 0.0).astype(BF16)
            for b in range(n_blk):
                rws = slice(b * BLOCK, (b + 1) * BLOCK)
                sps.append(jnp.dot(w, gvn[rws, jj * LANES:(jj + 1) * LANES], preferred_element_type=F32))
        return sps

    def gmlp_finish(c, gu, sps):
        for jj in range(2):
            grp = 2 * c + jj
            bias = bst_ref[:, grp:grp + 1]
            for b in range(n_blk):
                rws = slice(b * BLOCK, (b + 1) * BLOCK)
                g_ref[rws, grp * LANES:(grp + 1) * LANES] = (
                    gu[rws, jj * LANES:(jj + 1) * LANES] * (sps[jj * n_blk + b] + bias))

    def a_norm():
        m_ref[:, :ATTN_WIDTH] = _rmsnorm_rows(a_ref[...], ag_ref[...]).astype(BF16)

    def g_norm():
        m_ref[:, ATTN_WIDTH:] = _rmsnorm_rows(g_ref[...], gg_ref[...]).astype(BF16)

    def out_a(n):
        cols = slice(n * chunk, (n + 1) * chunk)
        o_ref[:, cols] = x_ref[:, cols] + jnp.dot(m_ref[:, :ATTN_WIDTH], wout_ref[:ATTN_WIDTH, cols],
                                                  preferred_element_type=F32)
        x_ref[:, cols] = xa_ref[:, cols]

    def out_g(n):
        cols = slice(n * chunk, (n + 1) * chunk)
        o_ref[:, cols] += jnp.dot(m_ref[:, ATTN_WIDTH:], wout_ref[ATTN_WIDTH:, cols],
                                  preferred_element_type=F32)

    n_q = ATTN_WIDTH // chunk
    n_g = GMLP_WIDTH // chunk
    n_out = o_ref.shape[1] // chunk
    units = [(b, hkv) for b in range(n_blk) for hkv in range(N_KV_HEADS)]
    assert len(units) == 2 * n_g

    x_norm()
    q_in = [heads_load(c * chunk) for c in range(n_q)]
    zk, k_sq = heads_load(col_k)
    zv = z_ref[:, col_v:col_v + KV_WIDTH]
    proj(0)
    q_ss = [heads_dot(sq) for _, sq in q_in]
    k_ss = heads_dot(k_sq)
    for c in range(n_q):
        q_finish(c, q_in[c][0], q_ss[c])
        if c % 2 == 1:
            proj((c + 1) // 2 * chunk)
    kv_finish(zk, k_ss, zv)
    proj((n_q - 1) * chunk)

    big = [col_k, col_v]
    for c in range(n_g):
        big += [col_gu + c * chunk, col_gv + c * chunk]
    scores, probs, act, scaled, spatial = {}, {}, {}, {}, {}
    for r in range(len(units) + 1):
        c = r // 2
        if r >= 1:
            weighted = att_values(*units[r - 1], probs[r - 1][0])
        if r < len(units):
            scores[r] = att_scores(*units[r])
        if r % 2 == 1:
            norm_ss = gmlp_norm_dot(act[c][2])
        elif c >= 1:
            spatial[c - 1] = gmlp_spatial_dots(c - 1, scaled[c - 1])
        if r >= 1:
            att_finish(*units[r - 1], weighted, probs[r - 1][1])
        if r < len(units):
            probs[r] = att_softmax(*units[r], scores[r])
        if r % 2 == 1:
            scaled[c] = gmlp_scale(c, act[c][1], norm_ss)
        else:
            if c >= 1:
                gmlp_finish(c - 1, act[c - 1][0], spatial[c - 1])
            if c < n_g:
                act[c] = gmlp_act(c)
        proj(big[r])
    carry_band()
    a_norm()
    proj(big[-1])
    for n in range(n_out):
        out_a(n)
    g_norm()
    for n in range(n_out):
        out_g(n)


def _mixer(x, pos, invf, gm, w_in, qg, kg, sinks, gvg, w_s, bst, ag, gg, w_out, *, seq_len):
    t, d = x.shape
    tb = MIXER_TOKEN_TILE
    assert t % tb == 0 and seq_len % tb == 0 and tb % BLOCK == 0
    n_tiles = t // tb

    def resident(shape):
        return pl.BlockSpec(shape, lambda i: (0,) * len(shape), pipeline_mode=pl.Buffered(1))

    def first_stage_tile(i):
        return jnp.minimum(i, n_tiles - 1)

    def second_stage_tile(i):
        return jnp.maximum(i - 1, 0)

    return pl.pallas_call(
        functools.partial(_mixer_kernel, tb=tb, tiles_per_seq=seq_len // tb),
        out_shape=jax.ShapeDtypeStruct((t, d), F32),
        grid=(n_tiles + 1,),
        in_specs=[
            pl.BlockSpec((tb, d), lambda i: (first_stage_tile(i), 0)),
            resident(pos.shape),
            resident(invf.shape),
            resident(gm.shape),
            resident(w_in.shape),
            resident(qg.shape),
            resident(kg.shape),
            pl.BlockSpec(memory_space=pltpu.SMEM),
            resident(gvg.shape),
            resident(w_s.shape),
            resident(bst.shape),
            resident(ag.shape),
            resident(gg.shape),
            resident(w_out.shape),
        ],
        out_specs=pl.BlockSpec((tb, d), lambda i: (second_stage_tile(i), 0)),
        scratch_shapes=[
            pltpu.VMEM((tb, d), BF16),
            pltpu.VMEM((tb, w_in.shape[1]), F32),
            pltpu.VMEM((tb, ATTN_WIDTH), BF16),
            pltpu.VMEM((2 * N_KV_HEADS, BLOCK + tb, LANES), BF16),
            pltpu.VMEM((2 * N_KV_HEADS, BLOCK + tb, LANES), BF16),
            pltpu.VMEM((tb, ATTN_WIDTH), F32),
            pltpu.VMEM((tb, GMLP_WIDTH), F32),
            pltpu.VMEM((tb, ATTN_WIDTH + GMLP_WIDTH), BF16),
            pltpu.VMEM((tb, d), F32),
        ],
        compiler_params=pltpu.CompilerParams(
            dimension_semantics=("arbitrary",),
            vmem_limit_bytes=V7X_VMEM_LIMIT_BYTES,
        ),
        name="mixer",
    )(x, pos, invf, gm, w_in, qg, kg, sinks, gvg, w_s, bst, ag, gg, w_out)


def kernel(x, positions, ffn1_norm, ffn1_w_gate, ffn1_w_up, ffn1_w_down, mix_norm, w_in,
           q_norm, k_norm, attn_sinks, gmlp_v_norm, gmlp_w_s, gmlp_b_s, attn_out_norm,
           gmlp_out_norm, w_out, ffn2_norm, ffn2_w_gate, ffn2_w_up, ffn2_w_down):
    b, s, d = x.shape
    depth = w_in.shape[0]
    xt = x.reshape(b * s, d)
    pos = positions.reshape(b * s // MIXER_TOKEN_TILE, 1, MIXER_TOKEN_TILE)
    invf = (ROPE_THETA ** (-jnp.arange(0, ROT_DIM, 2, dtype=F32) / ROT_DIM)).reshape(ROT_DIM // 2, 1)

    for l in range(depth):
        later = [w_in[l], w_out[l], ffn2_w_gate[l], ffn2_w_up[l], ffn2_w_down[l]]
        xt, (w_in_b, w_out_b, g2_b, u2_b, d2_b) = _ffn(
            xt, ffn1_norm[l].reshape(1, d), ffn1_w_gate[l].astype(BF16),
            ffn1_w_up[l].astype(BF16), ffn1_w_down[l].astype(BF16), casts=later)
        xt = _mixer(
            xt, pos, invf, mix_norm[l].reshape(1, d), w_in_b,
            (jnp.tile(q_norm[l], 2) * (HEAD_DIM ** -0.5)).reshape(1, LANES),
            jnp.tile(k_norm[l], 2).reshape(1, LANES),
            attn_sinks[l], gmlp_v_norm[l].reshape(1, GMLP_WIDTH), gmlp_w_s[l], gmlp_b_s[l].T,
            attn_out_norm[l].reshape(1, ATTN_WIDTH), gmlp_out_norm[l].reshape(1, GMLP_WIDTH),
            w_out_b, seq_len=s)
        xt, _ = _ffn(xt, ffn2_norm[l].reshape(1, d), g2_b, u2_b, d2_b)
    return xt.reshape(b, s, d)
```

```python
import functools

import jax
import jax.numpy as jnp
from jax import lax
from jax.experimental import pallas as pl
from jax.experimental.pallas import tpu as pltpu

HEAD_DIM = 64
N_Q_HEADS = 16
N_KV_HEADS = 4
ATTN_WIDTH = N_Q_HEADS * HEAD_DIM
KV_WIDTH = N_KV_HEADS * HEAD_DIM
BLOCK = 128
ROT_DIM = HEAD_DIM // 4
ROPE_THETA = 500000.0
GMLP_GROUP_DIM = 128
GMLP_WIDTH = 1024
N_GMLP_GROUPS = GMLP_WIDTH // GMLP_GROUP_DIM
EPS = 1e-6
NEG_INF = -1e30

LANES = 128
BF16_SUBLANES = 16
V7X_VMEM_LIMIT_BYTES = 60000 * 1024

FFN_TOKEN_TILE = 1024
FFN_FF_TILE = 512
FFN_NORM_CHUNKS = 4
CAST_BLOCK_COLS = 512
MIXER_TOKEN_TILE = 256
MIXER_CAST_COL_BLOCKS = 4
N_MIXER_INPUTS = 14

F32 = jnp.float32
BF16 = jnp.bfloat16


def _rmsnorm_rows(x, gain):
    y = x * lax.rsqrt(jnp.mean(x * x, axis=-1, keepdims=True) + EPS)
    return y * gain


def _ffn_kernel(*refs, n_casts):
    x_ref, gain_ref, wg_ref, wu_ref, wd_ref = refs[:5]
    cast_in = refs[5:5 + n_casts]
    o_ref = refs[5 + n_casts]
    cast_out = refs[6 + n_casts:6 + 2 * n_casts]
    h_ref = refs[6 + 2 * n_casts]
    f = pl.program_id(1)

    def activate(g, u):
        return (jax.nn.silu(g) * (0.5 * u)).astype(BF16)

    def side_casts():
        for src, dst in zip(cast_in, cast_out):
            dst[...] = src[...].astype(BF16)

    @pl.when(f == 0)
    def _():
        rows = h_ref.shape[0] // FFN_NORM_CHUNKS
        hs, gus = [], []

        def norm(r):
            sl = slice(r * rows, (r + 1) * rows)
            h = _rmsnorm_rows(x_ref[sl, :], gain_ref[...]).astype(BF16)
            h_ref[sl, :] = h
            return h

        def gate_up(h):
            return (jnp.dot(h, wg_ref[...], preferred_element_type=F32),
                    jnp.dot(h, wu_ref[...], preferred_element_type=F32))

        def down(r, a):
            sl = slice(r * rows, (r + 1) * rows)
            o_ref[sl, :] = x_ref[sl, :] + jnp.dot(a, wd_ref[...], preferred_element_type=F32)

        hs.append(norm(0))
        gus.append(gate_up(hs[0]))
        for r in range(FFN_NORM_CHUNKS):
            if r + 1 < FFN_NORM_CHUNKS:
                hs.append(norm(r + 1))
                gus.append(gate_up(hs[r + 1]))
            down(r, activate(*gus[r]))
        side_casts()

    @pl.when(f > 0)
    def _():
        h = h_ref[...]
        g = jnp.dot(h, wg_ref[...], preferred_element_type=F32)
        u = jnp.dot(h, wu_ref[...], preferred_element_type=F32)
        o_ref[...] += jnp.dot(activate(g, u), wd_ref[...], preferred_element_type=F32)
        side_casts()


def _ffn(x, gain, w_gate, w_up, w_down, casts=()):
    t, d = x.shape
    d_ff = w_gate.shape[1]
    tm, tf = FFN_TOKEN_TILE, FFN_FF_TILE
    assert t % tm == 0 and d_ff % tf == 0
    n_tiles, n_ff = t // tm, d_ff // tf

    cast_specs = []
    for w in casts:
        rows, cols = w.shape
        assert rows % (n_tiles * BF16_SUBLANES) == 0 and cols % CAST_BLOCK_COLS == 0
        last = cols // CAST_BLOCK_COLS - 1
        assert last < n_ff
        cast_specs.append(pl.BlockSpec((rows // n_tiles, CAST_BLOCK_COLS),
                                       lambda i, f, last=last: (i, jnp.minimum(f, last))))

    outs = pl.pallas_call(
        functools.partial(_ffn_kernel, n_casts=len(casts)),
        out_shape=[jax.ShapeDtypeStruct((t, d), F32)] + [jax.ShapeDtypeStruct(w.shape, BF16) for w in casts],
        grid=(n_tiles, n_ff),
        in_specs=[
            pl.BlockSpec((tm, d), lambda i, f: (i, 0)),
            pl.BlockSpec((1, d), lambda i, f: (0, 0)),
            pl.BlockSpec((d, tf), lambda i, f: (0, f)),
            pl.BlockSpec((d, tf), lambda i, f: (0, f)),
            pl.BlockSpec((tf, d), lambda i, f: (f, 0)),
        ] + cast_specs,
        out_specs=[pl.BlockSpec((tm, d), lambda i, f: (i, 0))] + cast_specs,
        scratch_shapes=[pltpu.VMEM((tm, d), BF16)],
        compiler_params=pltpu.CompilerParams(
            dimension_semantics=("arbitrary", "arbitrary"),
            vmem_limit_bytes=V7X_VMEM_LIMIT_BYTES,
        ),
        name="ffn",
    )(x, gain, w_gate, w_up, w_down, *casts)
    return outs[0], list(outs[1:])


def _block_diag_ones(size, group):
    r = lax.broadcasted_iota(jnp.int32, (size, size), 0) // group
    c = lax.broadcasted_iota(jnp.int32, (size, size), 1) // group
    return jnp.where(r == c, 1.0, 0.0).astype(BF16)


def _mixer_kernel(*refs, tb, tiles_per_seq, n_casts):
    (xa_ref, pos_ref, invf_ref, gm_ref, win_ref, qg_ref, kg_ref, sinks_ref,
     gvg_ref, ws_ref, bst_ref, ag_ref, gg_ref, wout_ref) = refs[:N_MIXER_INPUTS]
    cast_in = refs[N_MIXER_INPUTS:N_MIXER_INPUTS + n_casts]
    o_ref = refs[N_MIXER_INPUTS + n_casts]
    cast_out = refs[N_MIXER_INPUTS + n_casts + 1:N_MIXER_INPUTS + 2 * n_casts + 1]
    h_ref, z_ref, q_ref, kpad_ref, vpad_ref, a_ref, g_ref, m_ref, x_ref = refs[N_MIXER_INPUTS + 2 * n_casts + 1:]
    i = pl.program_id(0)
    n_blk = tb // BLOCK
    chunk = 2 * LANES
    half = ROT_DIM // 2
    col_k = ATTN_WIDTH
    col_v = col_k + KV_WIDTH
    col_gu = col_v + KV_WIDTH
    col_gv = col_gu + GMLP_WIDTH

    @pl.when(i == 0)
    def _():
        z_ref[...] = jnp.zeros_like(z_ref)
        x_ref[...] = jnp.zeros_like(x_ref)
        kpad_ref[:, :BLOCK, :] = jnp.zeros((2 * N_KV_HEADS, BLOCK, LANES), BF16)
        vpad_ref[:, :BLOCK, :] = jnp.zeros((2 * N_KV_HEADS, BLOCK, LANES), BF16)

    lane = lax.broadcasted_iota(jnp.int32, (1, LANES), 1)
    lo_half = lane < HEAD_DIM

    ang = invf_ref[...] * pos_ref[jnp.maximum(i - 1, 0)].astype(F32)
    cos, sin = jnp.cos(ang), jnp.sin(ang)
    rest_one = jnp.ones((HEAD_DIM - ROT_DIM, tb), F32)
    rest_zero = jnp.zeros((HEAD_DIM - ROT_DIM, tb), F32)
    zero = jnp.zeros((half, tb), F32)
    rope_c = jnp.concatenate([cos, cos, rest_one] * 2, axis=0).T
    rope_s1 = jnp.concatenate([-sin, zero, rest_zero] * 2, axis=0).T
    rope_s2 = jnp.concatenate([zero, sin, rest_zero] * 2, axis=0).T

    bd_head = _block_diag_ones(chunk, HEAD_DIM)
    bd_group = _block_diag_ones(chunk, GMLP_GROUP_DIM)

    def rope(t):
        return t * rope_c + pltpu.roll(t, LANES - half, 1) * rope_s1 + pltpu.roll(t, half, 1) * rope_s2

    def x_norm():
        h_ref[...] = _rmsnorm_rows(xa_ref[...], gm_ref[...]).astype(BF16)

    def proj(c0):
        cols = slice(c0, c0 + chunk)
        z_ref[:, cols] = jnp.dot(h_ref[...], win_ref[:, cols], preferred_element_type=F32)

    def heads_load(c0):
        z = z_ref[:, c0:c0 + chunk]
        return z, (z * z).astype(BF16)

    def heads_dot(sq):
        return jnp.dot(sq, bd_head, preferred_element_type=F32)

    def heads_finish(z, ss, gain):
        zn = z * lax.rsqrt(ss * (1.0 / HEAD_DIM) + EPS)
        return [rope(zn[:, j * LANES:(j + 1) * LANES] * gain) for j in range(2)]

    def q_finish(c, z, ss):
        s0, s1 = heads_finish(z, ss, qg_ref[...])
        q_ref[:, c * chunk:(c + 1) * chunk] = jnp.concatenate([s0, s1], axis=1).astype(BF16)

    def place_heads(slab, dst_ref, j):
        rolled = pltpu.roll(slab, HEAD_DIM, 1)
        zero_slab = jnp.zeros_like(slab)
        cur = slice(BLOCK, BLOCK + tb)
        dst_ref[4 * j + 0, cur, :] = jnp.where(lo_half, slab, zero_slab).astype(BF16)
        dst_ref[4 * j + 1, cur, :] = jnp.where(lo_half, zero_slab, rolled).astype(BF16)
        dst_ref[4 * j + 2, cur, :] = jnp.where(lo_half, rolled, zero_slab).astype(BF16)
        dst_ref[4 * j + 3, cur, :] = jnp.where(lo_half, zero_slab, slab).astype(BF16)

    def kv_finish(zk, ss, zv):
        k_slabs = heads_finish(zk, ss, kg_ref[...])
        for j in range(2):
            place_heads(k_slabs[j], kpad_ref, j)
            place_heads(zv[:, j * LANES:(j + 1) * LANES], vpad_ref, j)

    srow = lax.broadcasted_iota(jnp.int32, (2 * BLOCK, 4 * BLOCK), 0) % BLOCK
    scol = lax.broadcasted_iota(jnp.int32, (2 * BLOCK, 4 * BLOCK), 1) % (2 * BLOCK)
    row_first_slab = lax.broadcasted_iota(jnp.int32, (2 * BLOCK, 1), 0) < BLOCK
    seq_start = ((i + tiles_per_seq - 1) % tiles_per_seq) == 0
    ones_lo = jnp.broadcast_to(jnp.where(lo_half, 1.0, 0.0).astype(BF16), (2 * BLOCK, LANES))
    ones_hi = jnp.broadcast_to(jnp.where(lo_half, 0.0, 1.0).astype(BF16), (2 * BLOCK, LANES))

    def band_mask(b):
        prev_off = jnp.where(seq_start, BLOCK, 0) if b == 0 else 0
        return (((scol < BLOCK) & (scol > srow + prev_off))
                | ((scol >= BLOCK) & (scol - BLOCK <= srow)))

    def att_scores(b, hkv):
        qrows = slice(b * BLOCK, (b + 1) * BLOCK)
        krows = slice(b * BLOCK, (b + 2) * BLOCK)
        q2 = jnp.concatenate([q_ref[qrows, (2 * hkv) * LANES:(2 * hkv + 1) * LANES],
                              q_ref[qrows, (2 * hkv + 1) * LANES:(2 * hkv + 2) * LANES]], axis=0)
        kcat = jnp.concatenate([kpad_ref[2 * hkv, krows, :], kpad_ref[2 * hkv + 1, krows, :]], axis=0)
        return lax.dot_general(q2, kcat, (((1,), (1,)), ((), ())), preferred_element_type=F32)

    def att_softmax(b, hkv, s):
        s = jnp.where(band_mask(b), s, NEG_INF)
        sink_lo = jnp.where(row_first_slab, sinks_ref[4 * hkv], sinks_ref[4 * hkv + 2])
        sink_hi = jnp.where(row_first_slab, sinks_ref[4 * hkv + 1], sinks_ref[4 * hkv + 3])
        s_lo, s_hi = s[:, :2 * BLOCK], s[:, 2 * BLOCK:]
        m_lo = jnp.maximum(jnp.max(s_lo, axis=1, keepdims=True), sink_lo)
        m_hi = jnp.maximum(jnp.max(s_hi, axis=1, keepdims=True), sink_hi)
        p = jnp.concatenate([jnp.exp(s_lo - m_lo), jnp.exp(s_hi - m_hi)], axis=1).astype(BF16)
        sink_mass = jnp.where(lo_half, jnp.exp(sink_lo - m_lo), jnp.exp(sink_hi - m_hi))
        return p, sink_mass

    def att_values(b, hkv, p):
        krows = slice(b * BLOCK, (b + 2) * BLOCK)
        vcat = jnp.concatenate(
            [jnp.concatenate([vpad_ref[2 * hkv, krows, :], ones_lo], axis=1),
             jnp.concatenate([vpad_ref[2 * hkv + 1, krows, :], ones_hi], axis=1)], axis=0)
        return jnp.dot(p, vcat, preferred_element_type=F32)

    def att_finish(b, hkv, o2, sink_mass):
        qrows = slice(b * BLOCK, (b + 1) * BLOCK)
        out = o2[:, :LANES] / (o2[:, LANES:] + sink_mass)
        a_ref[qrows, (2 * hkv) * LANES:(2 * hkv + 1) * LANES] = out[:BLOCK]
        a_ref[qrows, (2 * hkv + 1) * LANES:(2 * hkv + 2) * LANES] = out[BLOCK:]

    def carry_band():
        last = slice(tb, tb + BLOCK)
        kpad_ref[:, :BLOCK, :] = kpad_ref[:, last, :]
        vpad_ref[:, :BLOCK, :] = vpad_ref[:, last, :]

    trow = lax.broadcasted_iota(jnp.int32, (BLOCK, BLOCK), 0)
    tcol = lax.broadcasted_iota(jnp.int32, (BLOCK, BLOCK), 1)
    causal = tcol <= trow

    def gmlp_act(c):
        gu = jax.nn.gelu(z_ref[:, col_gu + c * chunk:col_gu + (c + 1) * chunk])
        gv = jax.nn.gelu(z_ref[:, col_gv + c * chunk:col_gv + (c + 1) * chunk])
        return gu, gv, (gv * gv).astype(BF16)

    def gmlp_norm_dot(sq):
        return jnp.dot(sq, bd_group, preferred_element_type=F32)

    def gmlp_scale(c, gv, ss):
        gv = gv * lax.rsqrt(ss * (1.0 / GMLP_GROUP_DIM) + EPS)
        return (gv * gvg_ref[:, c * chunk:(c + 1) * chunk]).astype(BF16)

    def gmlp_spatial_dots(c, gvn):
        sps = []
        for jj in range(2):
            w = jnp.where(causal, ws_ref[2 * c + jj], 0.0).astype(BF16)
            for b in range(n_blk):
                rws = slice(b * BLOCK, (b + 1) * BLOCK)
                sps.append(jnp.dot(w, gvn[rws, jj * LANES:(jj + 1) * LANES], preferred_element_type=F32))
        return sps

    def gmlp_finish(c, gu, sps):
        for jj in range(2):
            grp = 2 * c + jj
            bias = bst_ref[:, grp:grp + 1]
            for b in range(n_blk):
                rws = slice(b * BLOCK, (b + 1) * BLOCK)
                g_ref[rws, grp * LANES:(grp + 1) * LANES] = (
                    gu[rws, jj * LANES:(jj + 1) * LANES] * (sps[jj * n_blk + b] + bias))

    def a_norm():
        m_ref[:, :ATTN_WIDTH] = _rmsnorm_rows(a_ref[...], ag_ref[...]).astype(BF16)

    def g_norm():
        m_ref[:, ATTN_WIDTH:] = _rmsnorm_rows(g_ref[...], gg_ref[...]).astype(BF16)

    def out_a(n):
        cols = slice(n * chunk, (n + 1) * chunk)
        o_ref[:, cols] = x_ref[:, cols] + jnp.dot(m_ref[:, :ATTN_WIDTH], wout_ref[:ATTN_WIDTH, cols],
                                                  preferred_element_type=F32)
        x_ref[:, cols] = xa_ref[:, cols]

    def out_g(n):
        cols = slice(n * chunk, (n + 1) * chunk)
        o_ref[:, cols] += jnp.dot(m_ref[:, ATTN_WIDTH:], wout_ref[ATTN_WIDTH:, cols],
                                  preferred_element_type=F32)

    n_q = ATTN_WIDTH // chunk
    n_g = GMLP_WIDTH // chunk
    n_out = o_ref.shape[1] // chunk
    units = [(b, hkv) for b in range(n_blk) for hkv in range(N_KV_HEADS)]
    assert len(units) == 2 * n_g

    x_norm()
    for src, dst in zip(cast_in, cast_out):
        dst[...] = src[...].astype(BF16)
    q_in = [heads_load(c * chunk) for c in range(n_q)]
    zk, k_sq = heads_load(col_k)
    zv = z_ref[:, col_v:col_v + KV_WIDTH]
    proj(0)
    q_ss = [heads_dot(sq) for _, sq in q_in]
    k_ss = heads_dot(k_sq)
    for c in range(n_q):
        q_finish(c, q_in[c][0], q_ss[c])
        if c % 2 == 1:
            proj((c + 1) // 2 * chunk)
    kv_finish(zk, k_ss, zv)
    proj((n_q - 1) * chunk)

    big = [col_k, col_v]
    for c in range(n_g):
        big += [col_gu + c * chunk, col_gv + c * chunk]
    scores, probs, act, scaled, spatial = {}, {}, {}, {}, {}
    for r in range(len(units) + 1):
        c = r // 2
        if r >= 1:
            weighted = att_values(*units[r - 1], probs[r - 1][0])
        if r < len(units):
            scores[r] = att_scores(*units[r])
        if r % 2 == 1:
            norm_ss = gmlp_norm_dot(act[c][2])
        elif c >= 1:
            spatial[c - 1] = gmlp_spatial_dots(c - 1, scaled[c - 1])
        if r >= 1:
            att_finish(*units[r - 1], weighted, probs[r - 1][1])
        if r < len(units):
            probs[r] = att_softmax(*units[r], scores[r])
        if r % 2 == 1:
            scaled[c] = gmlp_scale(c, act[c][1], norm_ss)
        else:
            if c >= 1:
                gmlp_finish(c - 1, act[c - 1][0], spatial[c - 1])
            if c < n_g:
                act[c] = gmlp_act(c)
        proj(big[r])
    carry_band()
    a_norm()
    proj(big[-1])
    for n in range(n_out):
        out_a(n)
    g_norm()
    for n in range(n_out):
        out_g(n)


def _mixer(x, pos, invf, gm, w_in, qg, kg, sinks, gvg, w_s, bst, ag, gg, w_out, *, seq_len, casts=()):
    t, d = x.shape
    tb = MIXER_TOKEN_TILE
    assert t % tb == 0 and seq_len % tb == 0 and tb % BLOCK == 0
    n_tiles = t // tb

    cast_specs = []
    for w in casts:
        rows, cols = w.shape
        nc = MIXER_CAST_COL_BLOCKS
        assert n_tiles % nc == 0
        nr = n_tiles // nc
        assert rows % (nr * BF16_SUBLANES) == 0 and cols % (nc * LANES) == 0
        cast_specs.append(pl.BlockSpec(
            (rows // nr, cols // nc),
            lambda i, nc=nc: (jnp.minimum(i, n_tiles - 1) // nc, jnp.minimum(i, n_tiles - 1) % nc)))

    def resident(shape):
        return pl.BlockSpec(shape, lambda i: (0,) * len(shape), pipeline_mode=pl.Buffered(1))

    def first_stage_tile(i):
        return jnp.minimum(i, n_tiles - 1)

    def second_stage_tile(i):
        return jnp.maximum(i - 1, 0)

    in_arrays = (x, pos, invf, gm, w_in, qg, kg, sinks, gvg, w_s, bst, ag, gg, w_out)
    assert len(in_arrays) == N_MIXER_INPUTS
    outs = pl.pallas_call(
        functools.partial(_mixer_kernel, tb=tb, tiles_per_seq=seq_len // tb, n_casts=len(casts)),
        out_shape=[jax.ShapeDtypeStruct((t, d), F32)] + [jax.ShapeDtypeStruct(w.shape, BF16) for w in casts],
        grid=(n_tiles + 1,),
        in_specs=[
            pl.BlockSpec((tb, d), lambda i: (first_stage_tile(i), 0)),
            resident(pos.shape),
            resident(invf.shape),
            resident(gm.shape),
            resident(w_in.shape),
            resident(qg.shape),
            resident(kg.shape),
            pl.BlockSpec(memory_space=pltpu.SMEM),
            resident(gvg.shape),
            resident(w_s.shape),
            resident(bst.shape),
            resident(ag.shape),
            resident(gg.shape),
            resident(w_out.shape),
        ] + cast_specs,
        out_specs=[pl.BlockSpec((tb, d), lambda i: (second_stage_tile(i), 0))] + cast_specs,
        scratch_shapes=[
            pltpu.VMEM((tb, d), BF16),
            pltpu.VMEM((tb, w_in.shape[1]), F32),
            pltpu.VMEM((tb, ATTN_WIDTH), BF16),
            pltpu.VMEM((2 * N_KV_HEADS, BLOCK + tb, LANES), BF16),
            pltpu.VMEM((2 * N_KV_HEADS, BLOCK + tb, LANES), BF16),
            pltpu.VMEM((tb, ATTN_WIDTH), F32),
            pltpu.VMEM((tb, GMLP_WIDTH), F32),
            pltpu.VMEM((tb, ATTN_WIDTH + GMLP_WIDTH), BF16),
            pltpu.VMEM((tb, d), F32),
        ],
        compiler_params=pltpu.CompilerParams(
            dimension_semantics=("arbitrary",),
            vmem_limit_bytes=V7X_VMEM_LIMIT_BYTES,
        ),
        name="mixer",
    )(*in_arrays, *casts)
    return outs[0], list(outs[1:])


def kernel(x, positions, ffn1_norm, ffn1_w_gate, ffn1_w_up, ffn1_w_down, mix_norm, w_in,
           q_norm, k_norm, attn_sinks, gmlp_v_norm, gmlp_w_s, gmlp_b_s, attn_out_norm,
           gmlp_out_norm, w_out, ffn2_norm, ffn2_w_gate, ffn2_w_up, ffn2_w_down):
    b, s, d = x.shape
    depth = w_in.shape[0]
    xt = x.reshape(b * s, d)
    pos = positions.reshape(b * s // MIXER_TOKEN_TILE, 1, MIXER_TOKEN_TILE)
    invf = (ROPE_THETA ** (-jnp.arange(0, ROT_DIM, 2, dtype=F32) / ROT_DIM)).reshape(ROT_DIM // 2, 1)

    for l in range(depth):
        xt, (w_in_b, w_out_b) = _ffn(
            xt, ffn1_norm[l].reshape(1, d), ffn1_w_gate[l].astype(BF16),
            ffn1_w_up[l].astype(BF16), ffn1_w_down[l].astype(BF16), casts=[w_in[l], w_out[l]])
        xt, (g2_b, u2_b, d2_b) = _mixer(
            xt, pos, invf, mix_norm[l].reshape(1, d), w_in_b,
            (jnp.tile(q_norm[l], 2) * (HEAD_DIM ** -0.5)).reshape(1, LANES),
            jnp.tile(k_norm[l], 2).reshape(1, LANES),
            attn_sinks[l], gmlp_v_norm[l].reshape(1, GMLP_WIDTH), gmlp_w_s[l], gmlp_b_s[l].T,
            attn_out_norm[l].reshape(1, ATTN_WIDTH), gmlp_out_norm[l].reshape(1, GMLP_WIDTH),
            w_out_b, seq_len=s, casts=[ffn2_w_gate[l], ffn2_w_up[l], ffn2_w_down[l]])
        xt, _ = _ffn(xt, ffn2_norm[l].reshape(1, d), g2_b, u2_b, d2_b)
    return xt.reshape(b, s, d)
```

```python
import functools

import jax
import jax.numpy as jnp
from jax import lax
from jax.experimental import pallas as pl
from jax.experimental.pallas import tpu as pltpu

HEAD_DIM = 64
N_Q_HEADS = 16
N_KV_HEADS = 4
ATTN_WIDTH = N_Q_HEADS * HEAD_DIM
KV_WIDTH = N_KV_HEADS * HEAD_DIM
BLOCK = 128
ROT_DIM = HEAD_DIM // 4
ROPE_THETA = 500000.0
GMLP_GROUP_DIM = 128
GMLP_WIDTH = 1024
N_GMLP_GROUPS = GMLP_WIDTH // GMLP_GROUP_DIM
EPS = 1e-6
NEG_INF = -1e30

LANES = 128
BF16_SUBLANES = 16
V7X_VMEM_LIMIT_BYTES = 60000 * 1024

FFN_TOKEN_TILE = 1024
FFN_FF_TILE = 512
FFN_NORM_CHUNKS = 4
MIXER_TOKEN_TILE = 256
MIXER_CAST_COL_BLOCKS = 4
N_MIXER_INPUTS = 14

F32 = jnp.float32
BF16 = jnp.bfloat16


def _rmsnorm_rows(x, gain):
    y = x * lax.rsqrt(jnp.mean(x * x, axis=-1, keepdims=True) + EPS)
    return y * gain


def _ffn_kernel(x_ref, gain_ref, wg_ref, wu_ref, wd_ref, o_ref, h_ref):
    f = pl.program_id(1)

    def activate(g, u):
        return (jax.nn.silu(g) * (0.5 * u)).astype(BF16)

    @pl.when(f == 0)
    def _():
        rows = h_ref.shape[0] // FFN_NORM_CHUNKS
        hs, gus = [], []

        def norm(r):
            sl = slice(r * rows, (r + 1) * rows)
            h = _rmsnorm_rows(x_ref[sl, :], gain_ref[...]).astype(BF16)
            h_ref[sl, :] = h
            return h

        def gate_up(h):
            return (jnp.dot(h, wg_ref[...], preferred_element_type=F32),
                    jnp.dot(h, wu_ref[...], preferred_element_type=F32))

        def down(r, a):
            sl = slice(r * rows, (r + 1) * rows)
            o_ref[sl, :] = x_ref[sl, :] + jnp.dot(a, wd_ref[...], preferred_element_type=F32)

        hs.append(norm(0))
        gus.append(gate_up(hs[0]))
        for r in range(FFN_NORM_CHUNKS):
            if r + 1 < FFN_NORM_CHUNKS:
                hs.append(norm(r + 1))
                gus.append(gate_up(hs[r + 1]))
            down(r, activate(*gus[r]))

    @pl.when(f > 0)
    def _():
        h = h_ref[...]
        g = jnp.dot(h, wg_ref[...], preferred_element_type=F32)
        u = jnp.dot(h, wu_ref[...], preferred_element_type=F32)
        o_ref[...] += jnp.dot(activate(g, u), wd_ref[...], preferred_element_type=F32)


def _ffn(x, gain, w_gate, w_up, w_down):
    t, d = x.shape
    d_ff = w_gate.shape[1]
    tm, tf = FFN_TOKEN_TILE, FFN_FF_TILE
    assert t % tm == 0 and d_ff % tf == 0
    return pl.pallas_call(
        _ffn_kernel,
        out_shape=jax.ShapeDtypeStruct((t, d), F32),
        grid=(t // tm, d_ff // tf),
        in_specs=[
            pl.BlockSpec((tm, d), lambda i, f: (i, 0)),
            pl.BlockSpec((1, d), lambda i, f: (0, 0)),
            pl.BlockSpec((d, tf), lambda i, f: (0, f)),
            pl.BlockSpec((d, tf), lambda i, f: (0, f)),
            pl.BlockSpec((tf, d), lambda i, f: (f, 0)),
        ],
        out_specs=pl.BlockSpec((tm, d), lambda i, f: (i, 0)),
        scratch_shapes=[pltpu.VMEM((tm, d), BF16)],
        compiler_params=pltpu.CompilerParams(
            dimension_semantics=("arbitrary", "arbitrary"),
            vmem_limit_bytes=V7X_VMEM_LIMIT_BYTES,
        ),
        name="ffn",
    )(x, gain, w_gate, w_up, w_down)


def _block_diag_ones(size, group):
    r = lax.broadcasted_iota(jnp.int32, (size, size), 0) // group
    c = lax.broadcasted_iota(jnp.int32, (size, size), 1) // group
    return jnp.where(r == c, 1.0, 0.0).astype(BF16)


def _mixer_kernel(*refs, tb, tiles_per_seq, n_casts):
    (xa_ref, pos_ref, invf_ref, gm_ref, win_ref, qg_ref, kg_ref, sinks_ref,
     gvg_ref, ws_ref, bst_ref, ag_ref, gg_ref, wout_ref) = refs[:N_MIXER_INPUTS]
    cast_in = refs[N_MIXER_INPUTS:N_MIXER_INPUTS + n_casts]
    o_ref = refs[N_MIXER_INPUTS + n_casts]
    cast_out = refs[N_MIXER_INPUTS + n_casts + 1:N_MIXER_INPUTS + 2 * n_casts + 1]
    h_ref, z_ref, q_ref, kpad_ref, vpad_ref, a_ref, g_ref, m_ref, x_ref = refs[N_MIXER_INPUTS + 2 * n_casts + 1:]
    i = pl.program_id(0)
    n_blk = tb // BLOCK
    chunk = 2 * LANES
    half = ROT_DIM // 2
    col_k = ATTN_WIDTH
    col_v = col_k + KV_WIDTH
    col_gu = col_v + KV_WIDTH
    col_gv = col_gu + GMLP_WIDTH

    @pl.when(i == 0)
    def _():
        z_ref[...] = jnp.zeros_like(z_ref)
        x_ref[...] = jnp.zeros_like(x_ref)
        kpad_ref[:, :BLOCK, :] = jnp.zeros((2 * N_KV_HEADS, BLOCK, LANES), BF16)
        vpad_ref[:, :BLOCK, :] = jnp.zeros((2 * N_KV_HEADS, BLOCK, LANES), BF16)

    lane = lax.broadcasted_iota(jnp.int32, (1, LANES), 1)
    lo_half = lane < HEAD_DIM

    ang = invf_ref[...] * pos_ref[jnp.maximum(i - 1, 0)].astype(F32)
    cos, sin = jnp.cos(ang), jnp.sin(ang)
    rest_one = jnp.ones((HEAD_DIM - ROT_DIM, tb), F32)
    rest_zero = jnp.zeros((HEAD_DIM - ROT_DIM, tb), F32)
    zero = jnp.zeros((half, tb), F32)
    rope_c = jnp.concatenate([cos, cos, rest_one] * 2, axis=0).T
    rope_s1 = jnp.concatenate([-sin, zero, rest_zero] * 2, axis=0).T
    rope_s2 = jnp.concatenate([zero, sin, rest_zero] * 2, axis=0).T

    bd_head = _block_diag_ones(chunk, HEAD_DIM)
    bd_group = _block_diag_ones(chunk, GMLP_GROUP_DIM)

    def rope(t):
        return t * rope_c + pltpu.roll(t, LANES - half, 1) * rope_s1 + pltpu.roll(t, half, 1) * rope_s2

    def x_norm():
        h_ref[...] = _rmsnorm_rows(xa_ref[...], gm_ref[...]).astype(BF16)

    def proj(c0):
        cols = slice(c0, c0 + chunk)
        z_ref[:, cols] = jnp.dot(h_ref[...], win_ref[:, cols], preferred_element_type=F32)

    def heads_load(c0):
        z = z_ref[:, c0:c0 + chunk]
        return z, (z * z).astype(BF16)

    def heads_dot(sq):
        return jnp.dot(sq, bd_head, preferred_element_type=F32)

    def heads_finish(z, ss, gain):
        zn = z * lax.rsqrt(ss * (1.0 / HEAD_DIM) + EPS)
        return [rope(zn[:, j * LANES:(j + 1) * LANES] * gain) for j in range(2)]

    def q_finish(c, z, ss):
        s0, s1 = heads_finish(z, ss, qg_ref[...])
        q_ref[:, c * chunk:(c + 1) * chunk] = jnp.concatenate([s0, s1], axis=1).astype(BF16)

    def place_heads(slab, dst_ref, j):
        rolled = pltpu.roll(slab, HEAD_DIM, 1)
        zero_slab = jnp.zeros_like(slab)
        cur = slice(BLOCK, BLOCK + tb)
        dst_ref[4 * j + 0, cur, :] = jnp.where(lo_half, slab, zero_slab).astype(BF16)
        dst_ref[4 * j + 1, cur, :] = jnp.where(lo_half, zero_slab, rolled).astype(BF16)
        dst_ref[4 * j + 2, cur, :] = jnp.where(lo_half, rolled, zero_slab).astype(BF16)
        dst_ref[4 * j + 3, cur, :] = jnp.where(lo_half, zero_slab, slab).astype(BF16)

    def kv_finish(zk, ss, zv):
        k_slabs = heads_finish(zk, ss, kg_ref[...])
        for j in range(2):
            place_heads(k_slabs[j], kpad_ref, j)
            place_heads(zv[:, j * LANES:(j + 1) * LANES], vpad_ref, j)

    srow = lax.broadcasted_iota(jnp.int32, (2 * BLOCK, 4 * BLOCK), 0) % BLOCK
    scol = lax.broadcasted_iota(jnp.int32, (2 * BLOCK, 4 * BLOCK), 1) % (2 * BLOCK)
    row_first_slab = lax.broadcasted_iota(jnp.int32, (2 * BLOCK, 1), 0) < BLOCK
    seq_start = ((i + tiles_per_seq - 1) % tiles_per_seq) == 0
    ones_lo = jnp.broadcast_to(jnp.where(lo_half, 1.0, 0.0).astype(BF16), (2 * BLOCK, LANES))
    ones_hi = jnp.broadcast_to(jnp.where(lo_half, 0.0, 1.0).astype(BF16), (2 * BLOCK, LANES))

    def band_mask(b):
        prev_off = jnp.where(seq_start, BLOCK, 0) if b == 0 else 0
        return (((scol < BLOCK) & (scol > srow + prev_off))
                | ((scol >= BLOCK) & (scol - BLOCK <= srow)))

    def att_scores(b, hkv):
        qrows = slice(b * BLOCK, (b + 1) * BLOCK)
        krows = slice(b * BLOCK, (b + 2) * BLOCK)
        q2 = jnp.concatenate([q_ref[qrows, (2 * hkv) * LANES:(2 * hkv + 1) * LANES],
                              q_ref[qrows, (2 * hkv + 1) * LANES:(2 * hkv + 2) * LANES]], axis=0)
        kcat = jnp.concatenate([kpad_ref[2 * hkv, krows, :], kpad_ref[2 * hkv + 1, krows, :]], axis=0)
        return lax.dot_general(q2, kcat, (((1,), (1,)), ((), ())), preferred_element_type=F32)

    def att_softmax(b, hkv, s):
        s = jnp.where(band_mask(b), s, NEG_INF)
        sink_lo = jnp.where(row_first_slab, sinks_ref[4 * hkv], sinks_ref[4 * hkv + 2])
        sink_hi = jnp.where(row_first_slab, sinks_ref[4 * hkv + 1], sinks_ref[4 * hkv + 3])
        s_lo, s_hi = s[:, :2 * BLOCK], s[:, 2 * BLOCK:]
        m_lo = jnp.maximum(jnp.max(s_lo, axis=1, keepdims=True), sink_lo)
        m_hi = jnp.maximum(jnp.max(s_hi, axis=1, keepdims=True), sink_hi)
        p = jnp.concatenate([jnp.exp(s_lo - m_lo), jnp.exp(s_hi - m_hi)], axis=1).astype(BF16)
        sink_mass = jnp.where(lo_half, jnp.exp(sink_lo - m_lo), jnp.exp(sink_hi - m_hi))
        return p, sink_mass

    def att_values(b, hkv, p):
        krows = slice(b * BLOCK, (b + 2) * BLOCK)
        vcat = jnp.concatenate(
            [jnp.concatenate([vpad_ref[2 * hkv, krows, :], ones_lo], axis=1),
             jnp.concatenate([vpad_ref[2 * hkv + 1, krows, :], ones_hi], axis=1)], axis=0)
        return jnp.dot(p, vcat, preferred_element_type=F32)

    def att_finish(b, hkv, o2, sink_mass):
        qrows = slice(b * BLOCK, (b + 1) * BLOCK)
        out = o2[:, :LANES] / (o2[:, LANES:] + sink_mass)
        a_ref[qrows, (2 * hkv) * LANES:(2 * hkv + 1) * LANES] = out[:BLOCK]
        a_ref[qrows, (2 * hkv + 1) * LANES:(2 * hkv + 2) * LANES] = out[BLOCK:]

    def carry_band():
        last = slice(tb, tb + BLOCK)
        kpad_ref[:, :BLOCK, :] = kpad_ref[:, last, :]
        vpad_ref[:, :BLOCK, :] = vpad_ref[:, last, :]

    trow = lax.broadcasted_iota(jnp.int32, (BLOCK, BLOCK), 0)
    tcol = lax.broadcasted_iota(jnp.int32, (BLOCK, BLOCK), 1)
    causal = tcol <= trow

    def gmlp_act(c):
        gu = jax.nn.gelu(z_ref[:, col_gu + c * chunk:col_gu + (c + 1) * chunk])
        gv = jax.nn.gelu(z_ref[:, col_gv + c * chunk:col_gv + (c + 1) * chunk])
        return gu, gv, (gv * gv).astype(BF16)

    def gmlp_norm_dot(sq):
        return jnp.dot(sq, bd_group, preferred_element_type=F32)

    def gmlp_scale(c, gv, ss):
        gv = gv * lax.rsqrt(ss * (1.0 / GMLP_GROUP_DIM) + EPS)
        return (gv * gvg_ref[:, c * chunk:(c + 1) * chunk]).astype(BF16)

    def gmlp_spatial_dots(c, gvn):
        sps = []
        for jj in range(2):
            w = jnp.where(causal, ws_ref[2 * c + jj], 0.0).astype(BF16)
            for b in range(n_blk):
                rws = slice(b * BLOCK, (b + 1) * BLOCK)
                sps.append(jnp.dot(w, gvn[rws, jj * LANES:(jj + 1) * LANES], preferred_element_type=F32))
        return sps

    def gmlp_finish(c, gu, sps):
        for jj in range(2):
            grp = 2 * c + jj
            bias = bst_ref[:, grp:grp + 1]
            for b in range(n_blk):
                rws = slice(b * BLOCK, (b + 1) * BLOCK)
                g_ref[rws, grp * LANES:(grp + 1) * LANES] = (
                    gu[rws, jj * LANES:(jj + 1) * LANES] * (sps[jj * n_blk + b] + bias))

    def a_norm():
        m_ref[:, :ATTN_WIDTH] = _rmsnorm_rows(a_ref[...], ag_ref[...]).astype(BF16)

    def g_norm():
        m_ref[:, ATTN_WIDTH:] = _rmsnorm_rows(g_ref[...], gg_ref[...]).astype(BF16)

    def out_a(n):
        cols = slice(n * chunk, (n + 1) * chunk)
        o_ref[:, cols] = x_ref[:, cols] + jnp.dot(m_ref[:, :ATTN_WIDTH], wout_ref[:ATTN_WIDTH, cols],
                                                  preferred_element_type=F32)
        x_ref[:, cols] = xa_ref[:, cols]

    def out_g(n):
        cols = slice(n * chunk, (n + 1) * chunk)
        o_ref[:, cols] += jnp.dot(m_ref[:, ATTN_WIDTH:], wout_ref[ATTN_WIDTH:, cols],
                                  preferred_element_type=F32)

    n_q = ATTN_WIDTH // chunk
    n_g = GMLP_WIDTH // chunk
    n_out = o_ref.shape[1] // chunk
    units = [(b, hkv) for b in range(n_blk) for hkv in range(N_KV_HEADS)]
    assert len(units) == 2 * n_g

    x_norm()
    for src, dst in zip(cast_in, cast_out):
        dst[...] = src[...].astype(BF16)
    q_in = [heads_load(c * chunk) for c in range(n_q)]
    zk, k_sq = heads_load(col_k)
    zv = z_ref[:, col_v:col_v + KV_WIDTH]
    proj(0)
    q_ss = [heads_dot(sq) for _, sq in q_in]
    k_ss = heads_dot(k_sq)
    for c in range(n_q):
        q_finish(c, q_in[c][0], q_ss[c])
        if c % 2 == 1:
            proj((c + 1) // 2 * chunk)
    kv_finish(zk, k_ss, zv)
    proj((n_q - 1) * chunk)

    big = [col_k, col_v]
    for c in range(n_g):
        big += [col_gu + c * chunk, col_gv + c * chunk]
    scores, probs, act, scaled, spatial = {}, {}, {}, {}, {}
    for r in range(len(units) + 1):
        c = r // 2
        if r >= 1:
            weighted = att_values(*units[r - 1], probs[r - 1][0])
        if r < len(units):
            scores[r] = att_scores(*units[r])
        if r % 2 == 1:
            norm_ss = gmlp_norm_dot(act[c][2])
        elif c >= 1:
            spatial[c - 1] = gmlp_spatial_dots(c - 1, scaled[c - 1])
        if r >= 1:
            att_finish(*units[r - 1], weighted, probs[r - 1][1])
        if r < len(units):
            probs[r] = att_softmax(*units[r], scores[r])
        if r % 2 == 1:
            scaled[c] = gmlp_scale(c, act[c][1], norm_ss)
        else:
            if c >= 1:
                gmlp_finish(c - 1, act[c - 1][0], spatial[c - 1])
            if c < n_g:
                act[c] = gmlp_act(c)
        proj(big[r])
    carry_band()
    a_norm()
    proj(big[-1])
    for n in range(n_out):
        out_a(n)
    g_norm()
    for n in range(n_out):
        out_g(n)


def _mixer(x, pos, invf, gm, w_in, qg, kg, sinks, gvg, w_s, bst, ag, gg, w_out, *, seq_len, casts=()):
    t, d = x.shape
    tb = MIXER_TOKEN_TILE
    assert t % tb == 0 and seq_len % tb == 0 and tb % BLOCK == 0
    n_tiles = t // tb

    cast_specs = []
    for w in casts:
        rows, cols = w.shape
        nc = MIXER_CAST_COL_BLOCKS
        assert n_tiles % nc == 0
        nr = n_tiles // nc
        assert rows % (nr * BF16_SUBLANES) == 0 and cols % (nc * LANES) == 0
        cast_specs.append(pl.BlockSpec(
            (rows // nr, cols // nc),
            lambda i, nc=nc: (jnp.minimum(i, n_tiles - 1) // nc, jnp.minimum(i, n_tiles - 1) % nc)))

    def resident(shape):
        return pl.BlockSpec(shape, lambda i: (0,) * len(shape), pipeline_mode=pl.Buffered(1))

    def first_stage_tile(i):
        return jnp.minimum(i, n_tiles - 1)

    def second_stage_tile(i):
        return jnp.maximum(i - 1, 0)

    in_arrays = (x, pos, invf, gm, w_in, qg, kg, sinks, gvg, w_s, bst, ag, gg, w_out)
    assert len(in_arrays) == N_MIXER_INPUTS
    outs = pl.pallas_call(
        functools.partial(_mixer_kernel, tb=tb, tiles_per_seq=seq_len // tb, n_casts=len(casts)),
        out_shape=[jax.ShapeDtypeStruct((t, d), F32)] + [jax.ShapeDtypeStruct(w.shape, BF16) for w in casts],
        grid=(n_tiles + 1,),
        in_specs=[
            pl.BlockSpec((tb, d), lambda i: (first_stage_tile(i), 0)),
            resident(pos.shape),
            resident(invf.shape),
            resident(gm.shape),
            resident(w_in.shape),
            resident(qg.shape),
            resident(kg.shape),
            pl.BlockSpec(memory_space=pltpu.SMEM),
            resident(gvg.shape),
            resident(w_s.shape),
            resident(bst.shape),
            resident(ag.shape),
            resident(gg.shape),
            resident(w_out.shape),
        ] + cast_specs,
        out_specs=[pl.BlockSpec((tb, d), lambda i: (second_stage_tile(i), 0))] + cast_specs,
        scratch_shapes=[
            pltpu.VMEM((tb, d), BF16),
            pltpu.VMEM((tb, w_in.shape[1]), F32),
            pltpu.VMEM((tb, ATTN_WIDTH), BF16),
            pltpu.VMEM((2 * N_KV_HEADS, BLOCK + tb, LANES), BF16),
            pltpu.VMEM((2 * N_KV_HEADS, BLOCK + tb, LANES), BF16),
            pltpu.VMEM((tb, ATTN_WIDTH), F32),
            pltpu.VMEM((tb, GMLP_WIDTH), F32),
            pltpu.VMEM((tb, ATTN_WIDTH + GMLP_WIDTH), BF16),
            pltpu.VMEM((tb, d), F32),
        ],
        compiler_params=pltpu.CompilerParams(
            dimension_semantics=("arbitrary",),
            vmem_limit_bytes=V7X_VMEM_LIMIT_BYTES,
        ),
        name="mixer",
    )(*in_arrays, *casts)
    return outs[0], list(outs[1:])


def kernel(x, positions, ffn1_norm, ffn1_w_gate, ffn1_w_up, ffn1_w_down, mix_norm, w_in,
           q_norm, k_norm, attn_sinks, gmlp_v_norm, gmlp_w_s, gmlp_b_s, attn_out_norm,
           gmlp_out_norm, w_out, ffn2_norm, ffn2_w_gate, ffn2_w_up, ffn2_w_down):
    b, s, d = x.shape
    depth = w_in.shape[0]
    xt = x.reshape(b * s, d)
    pos = positions.reshape(b * s // MIXER_TOKEN_TILE, 1, MIXER_TOKEN_TILE)
    invf = (ROPE_THETA ** (-jnp.arange(0, ROT_DIM, 2, dtype=F32) / ROT_DIM)).reshape(ROT_DIM // 2, 1)

    for l in range(depth):
        xt = _ffn(xt, ffn1_norm[l].reshape(1, d), ffn1_w_gate[l].astype(BF16),
                  ffn1_w_up[l].astype(BF16), ffn1_w_down[l].astype(BF16))
        xt, (g2_b, u2_b, d2_b) = _mixer(
            xt, pos, invf, mix_norm[l].reshape(1, d), w_in[l].astype(BF16),
            (jnp.tile(q_norm[l], 2) * (HEAD_DIM ** -0.5)).reshape(1, LANES),
            jnp.tile(k_norm[l], 2).reshape(1, LANES),
            attn_sinks[l], gmlp_v_norm[l].reshape(1, GMLP_WIDTH), gmlp_w_s[l], gmlp_b_s[l].T,
            attn_out_norm[l].reshape(1, ATTN_WIDTH), gmlp_out_norm[l].reshape(1, GMLP_WIDTH),
            w_out[l].astype(BF16), seq_len=s, casts=[ffn2_w_gate[l], ffn2_w_up[l], ffn2_w_down[l]])
        xt = _ffn(xt, ffn2_norm[l].reshape(1, d), g2_b, u2_b, d2_b)
    return xt.reshape(b, s, d)
```

```python
import functools

import jax
import jax.numpy as jnp
from jax import lax
from jax.experimental import pallas as pl
from jax.experimental.pallas import tpu as pltpu

HEAD_DIM = 64
N_Q_HEADS = 16
N_KV_HEADS = 4
ATTN_WIDTH = N_Q_HEADS * HEAD_DIM
KV_WIDTH = N_KV_HEADS * HEAD_DIM
BLOCK = 128
ROT_DIM = HEAD_DIM // 4
ROPE_THETA = 500000.0
GMLP_GROUP_DIM = 128
GMLP_WIDTH = 1024
N_GMLP_GROUPS = GMLP_WIDTH // GMLP_GROUP_DIM
EPS = 1e-6
NEG_INF = -1e30

LANES = 128
BF16_SUBLANES = 16
V7X_VMEM_LIMIT_BYTES = 60000 * 1024

FFN_TOKEN_TILE = 1024
FFN_FF_TILE = 512
FFN_NORM_CHUNKS = 4
MIXER_TOKEN_TILE = 256
MIXER_CAST_COL_BLOCKS = 4
N_MIXER_INPUTS = 14

F32 = jnp.float32
BF16 = jnp.bfloat16


def _rmsnorm_rows(x, gain):
    y = x * lax.rsqrt(jnp.mean(x * x, axis=-1, keepdims=True) + EPS)
    return y * gain


def _ffn_kernel(x_ref, gain_ref, wg_ref, wu_ref, wd_ref, o_ref, h_ref):
    f = pl.program_id(1)

    def activate(g, u):
        return (jax.nn.silu(g) * (0.5 * u)).astype(BF16)

    @pl.when(f == 0)
    def _():
        rows = h_ref.shape[0] // FFN_NORM_CHUNKS
        hs, gus = [], []

        def norm(r):
            sl = slice(r * rows, (r + 1) * rows)
            h = _rmsnorm_rows(x_ref[sl, :], gain_ref[...]).astype(BF16)
            h_ref[sl, :] = h
            return h

        def gate_up(h):
            return (jnp.dot(h, wg_ref[...], preferred_element_type=F32),
                    jnp.dot(h, wu_ref[...], preferred_element_type=F32))

        def down(r, a):
            sl = slice(r * rows, (r + 1) * rows)
            o_ref[sl, :] = x_ref[sl, :] + jnp.dot(a, wd_ref[...], preferred_element_type=F32)

        hs.append(norm(0))
        gus.append(gate_up(hs[0]))
        for r in range(FFN_NORM_CHUNKS):
            if r + 1 < FFN_NORM_CHUNKS:
                hs.append(norm(r + 1))
                gus.append(gate_up(hs[r + 1]))
            down(r, activate(*gus[r]))

    @pl.when(f > 0)
    def _():
        h = h_ref[...]
        g = jnp.dot(h, wg_ref[...], preferred_element_type=F32)
        u = jnp.dot(h, wu_ref[...], preferred_element_type=F32)
        o_ref[...] += jnp.dot(activate(g, u), wd_ref[...], preferred_element_type=F32)


def _ffn(x, gain, w_gate, w_up, w_down):
    t, d = x.shape
    d_ff = w_gate.shape[1]
    tm, tf = FFN_TOKEN_TILE, FFN_FF_TILE
    assert t % tm == 0 and d_ff % tf == 0
    return pl.pallas_call(
        _ffn_kernel,
        out_shape=jax.ShapeDtypeStruct((t, d), F32),
        grid=(t // tm, d_ff // tf),
        in_specs=[
            pl.BlockSpec((tm, d), lambda i, f: (i, 0)),
            pl.BlockSpec((1, d), lambda i, f: (0, 0)),
            pl.BlockSpec((d, tf), lambda i, f: (0, f)),
            pl.BlockSpec((d, tf), lambda i, f: (0, f)),
            pl.BlockSpec((tf, d), lambda i, f: (f, 0)),
        ],
        out_specs=pl.BlockSpec((tm, d), lambda i, f: (i, 0)),
        scratch_shapes=[pltpu.VMEM((tm, d), BF16)],
        compiler_params=pltpu.CompilerParams(
            dimension_semantics=("arbitrary", "arbitrary"),
            vmem_limit_bytes=V7X_VMEM_LIMIT_BYTES,
        ),
        name="ffn",
    )(x, gain, w_gate, w_up, w_down)


def _block_diag_ones(size, group):
    r = lax.broadcasted_iota(jnp.int32, (size, size), 0) // group
    c = lax.broadcasted_iota(jnp.int32, (size, size), 1) // group
    return jnp.where(r == c, 1.0, 0.0).astype(BF16)


def _mixer_kernel(*refs, tb, tiles_per_seq, n_casts):
    i = pl.program_id(0)
    last = pl.num_programs(0) - 1
    step = functools.partial(_mixer_step, refs, tb=tb, tiles_per_seq=tiles_per_seq, n_casts=n_casts)
    pl.when(i == 0)(functools.partial(step, first_stage=True, second_stage=False))
    pl.when(jnp.logical_and(i > 0, i < last))(functools.partial(step, first_stage=True, second_stage=True))
    pl.when(i == last)(functools.partial(step, first_stage=False, second_stage=True))


def _mixer_step(refs, *, tb, tiles_per_seq, n_casts, first_stage, second_stage):
    (xa_ref, pos_ref, invf_ref, gm_ref, win_ref, qg_ref, kg_ref, sinks_ref,
     gvg_ref, ws_ref, bst_ref, ag_ref, gg_ref, wout_ref) = refs[:N_MIXER_INPUTS]
    cast_in = refs[N_MIXER_INPUTS:N_MIXER_INPUTS + n_casts]
    o_ref = refs[N_MIXER_INPUTS + n_casts]
    cast_out = refs[N_MIXER_INPUTS + n_casts + 1:N_MIXER_INPUTS + 2 * n_casts + 1]
    h_ref, z_ref, q_ref, kpad_ref, vpad_ref, a_ref, g_ref, m_ref, x_ref = refs[N_MIXER_INPUTS + 2 * n_casts + 1:]
    i = pl.program_id(0)
    n_blk = tb // BLOCK
    chunk = 2 * LANES
    half = ROT_DIM // 2
    col_k = ATTN_WIDTH
    col_v = col_k + KV_WIDTH
    col_gu = col_v + KV_WIDTH
    col_gv = col_gu + GMLP_WIDTH

    if not second_stage:
        kpad_ref[:, :BLOCK, :] = jnp.zeros((2 * N_KV_HEADS, BLOCK, LANES), BF16)
        vpad_ref[:, :BLOCK, :] = jnp.zeros((2 * N_KV_HEADS, BLOCK, LANES), BF16)

    lane = lax.broadcasted_iota(jnp.int32, (1, LANES), 1)
    lo_half = lane < HEAD_DIM

    ang = invf_ref[...] * pos_ref[jnp.maximum(i - 1, 0)].astype(F32)
    cos, sin = jnp.cos(ang), jnp.sin(ang)
    rest_one = jnp.ones((HEAD_DIM - ROT_DIM, tb), F32)
    rest_zero = jnp.zeros((HEAD_DIM - ROT_DIM, tb), F32)
    zero = jnp.zeros((half, tb), F32)
    rope_c = jnp.concatenate([cos, cos, rest_one] * 2, axis=0).T
    rope_s1 = jnp.concatenate([-sin, zero, rest_zero] * 2, axis=0).T
    rope_s2 = jnp.concatenate([zero, sin, rest_zero] * 2, axis=0).T

    bd_head = _block_diag_ones(chunk, HEAD_DIM)
    bd_group = _block_diag_ones(chunk, GMLP_GROUP_DIM)

    def rope(t):
        return t * rope_c + pltpu.roll(t, LANES - half, 1) * rope_s1 + pltpu.roll(t, half, 1) * rope_s2

    def x_norm():
        h_ref[...] = _rmsnorm_rows(xa_ref[...], gm_ref[...]).astype(BF16)

    def proj(c0):
        cols = slice(c0, c0 + chunk)
        z_ref[:, cols] = jnp.dot(h_ref[...], win_ref[:, cols], preferred_element_type=F32)

    def heads_load(c0):
        z = z_ref[:, c0:c0 + chunk]
        return z, (z * z).astype(BF16)

    def heads_dot(sq):
        return jnp.dot(sq, bd_head, preferred_element_type=F32)

    def heads_finish(z, ss, gain):
        zn = z * lax.rsqrt(ss * (1.0 / HEAD_DIM) + EPS)
        return [rope(zn[:, j * LANES:(j + 1) * LANES] * gain) for j in range(2)]

    def q_finish(c, z, ss):
        s0, s1 = heads_finish(z, ss, qg_ref[...])
        q_ref[:, c * chunk:(c + 1) * chunk] = jnp.concatenate([s0, s1], axis=1).astype(BF16)

    def place_heads(slab, dst_ref, j):
        rolled = pltpu.roll(slab, HEAD_DIM, 1)
        zero_slab = jnp.zeros_like(slab)
        cur = slice(BLOCK, BLOCK + tb)
        dst_ref[4 * j + 0, cur, :] = jnp.where(lo_half, slab, zero_slab).astype(BF16)
        dst_ref[4 * j + 1, cur, :] = jnp.where(lo_half, zero_slab, rolled).astype(BF16)
        dst_ref[4 * j + 2, cur, :] = jnp.where(lo_half, rolled, zero_slab).astype(BF16)
        dst_ref[4 * j + 3, cur, :] = jnp.where(lo_half, zero_slab, slab).astype(BF16)

    def kv_finish(zk, ss, zv):
        k_slabs = heads_finish(zk, ss, kg_ref[...])
        for j in range(2):
            place_heads(k_slabs[j], kpad_ref, j)
            place_heads(zv[:, j * LANES:(j + 1) * LANES], vpad_ref, j)

    srow = lax.broadcasted_iota(jnp.int32, (2 * BLOCK, 4 * BLOCK), 0) % BLOCK
    scol = lax.broadcasted_iota(jnp.int32, (2 * BLOCK, 4 * BLOCK), 1) % (2 * BLOCK)
    row_first_slab = lax.broadcasted_iota(jnp.int32, (2 * BLOCK, 1), 0) < BLOCK
    seq_start = ((i + tiles_per_seq - 1) % tiles_per_seq) == 0
    ones_lo = jnp.broadcast_to(jnp.where(lo_half, 1.0, 0.0).astype(BF16), (2 * BLOCK, LANES))
    ones_hi = jnp.broadcast_to(jnp.where(lo_half, 0.0, 1.0).astype(BF16), (2 * BLOCK, LANES))

    def band_mask(b):
        prev_off = jnp.where(seq_start, BLOCK, 0) if b == 0 else 0
        return (((scol < BLOCK) & (scol > srow + prev_off))
                | ((scol >= BLOCK) & (scol - BLOCK <= srow)))

    def att_scores(b, hkv):
        qrows = slice(b * BLOCK, (b + 1) * BLOCK)
        krows = slice(b * BLOCK, (b + 2) * BLOCK)
        q2 = jnp.concatenate([q_ref[qrows, (2 * hkv) * LANES:(2 * hkv + 1) * LANES],
                              q_ref[qrows, (2 * hkv + 1) * LANES:(2 * hkv + 2) * LANES]], axis=0)
        kcat = jnp.concatenate([kpad_ref[2 * hkv, krows, :], kpad_ref[2 * hkv + 1, krows, :]], axis=0)
        return lax.dot_general(q2, kcat, (((1,), (1,)), ((), ())), preferred_element_type=F32)

    def att_softmax(b, hkv, s):
        s = jnp.where(band_mask(b), s, NEG_INF)
        sink_lo = jnp.where(row_first_slab, sinks_ref[4 * hkv], sinks_ref[4 * hkv + 2])
        sink_hi = jnp.where(row_first_slab, sinks_ref[4 * hkv + 1], sinks_ref[4 * hkv + 3])
        s_lo, s_hi = s[:, :2 * BLOCK], s[:, 2 * BLOCK:]
        m_lo = jnp.maximum(jnp.max(s_lo, axis=1, keepdims=True), sink_lo)
        m_hi = jnp.maximum(jnp.max(s_hi, axis=1, keepdims=True), sink_hi)
        p = jnp.concatenate([jnp.exp(s_lo - m_lo), jnp.exp(s_hi - m_hi)], axis=1).astype(BF16)
        sink_mass = jnp.where(lo_half, jnp.exp(sink_lo - m_lo), jnp.exp(sink_hi - m_hi))
        return p, sink_mass

    def att_values(b, hkv, p):
        krows = slice(b * BLOCK, (b + 2) * BLOCK)
        vcat = jnp.concatenate(
            [jnp.concatenate([vpad_ref[2 * hkv, krows, :], ones_lo], axis=1),
             jnp.concatenate([vpad_ref[2 * hkv + 1, krows, :], ones_hi], axis=1)], axis=0)
        return jnp.dot(p, vcat, preferred_element_type=F32)

    def att_finish(b, hkv, o2, sink_mass):
        qrows = slice(b * BLOCK, (b + 1) * BLOCK)
        out = o2[:, :LANES] / (o2[:, LANES:] + sink_mass)
        a_ref[qrows, (2 * hkv) * LANES:(2 * hkv + 1) * LANES] = out[:BLOCK]
        a_ref[qrows, (2 * hkv + 1) * LANES:(2 * hkv + 2) * LANES] = out[BLOCK:]

    def carry_band():
        last = slice(tb, tb + BLOCK)
        kpad_ref[:, :BLOCK, :] = kpad_ref[:, last, :]
        vpad_ref[:, :BLOCK, :] = vpad_ref[:, last, :]

    trow = lax.broadcasted_iota(jnp.int32, (BLOCK, BLOCK), 0)
    tcol = lax.broadcasted_iota(jnp.int32, (BLOCK, BLOCK), 1)
    causal = tcol <= trow

    def gmlp_act(c):
        gu = jax.nn.gelu(z_ref[:, col_gu + c * chunk:col_gu + (c + 1) * chunk])
        gv = jax.nn.gelu(z_ref[:, col_gv + c * chunk:col_gv + (c + 1) * chunk])
        return gu, gv, (gv * gv).astype(BF16)

    def gmlp_norm_dot(sq):
        return jnp.dot(sq, bd_group, preferred_element_type=F32)

    def gmlp_scale(c, gv, ss):
        gv = gv * lax.rsqrt(ss * (1.0 / GMLP_GROUP_DIM) + EPS)
        return (gv * gvg_ref[:, c * chunk:(c + 1) * chunk]).astype(BF16)

    def gmlp_spatial_dots(c, gvn):
        sps = []
        for jj in range(2):
            w = jnp.where(causal, ws_ref[2 * c + jj], 0.0).astype(BF16)
            for b in range(n_blk):
                rws = slice(b * BLOCK, (b + 1) * BLOCK)
                sps.append(jnp.dot(w, gvn[rws, jj * LANES:(jj + 1) * LANES], preferred_element_type=F32))
        return sps

    def gmlp_finish(c, gu, sps):
        for jj in range(2):
            grp = 2 * c + jj
            bias = bst_ref[:, grp:grp + 1]
            for b in range(n_blk):
                rws = slice(b * BLOCK, (b + 1) * BLOCK)
                g_ref[rws, grp * LANES:(grp + 1) * LANES] = (
                    gu[rws, jj * LANES:(jj + 1) * LANES] * (sps[jj * n_blk + b] + bias))

    def a_norm():
        m_ref[:, :ATTN_WIDTH] = _rmsnorm_rows(a_ref[...], ag_ref[...]).astype(BF16)

    def g_norm():
        m_ref[:, ATTN_WIDTH:] = _rmsnorm_rows(g_ref[...], gg_ref[...]).astype(BF16)

    def out_a(n):
        cols = slice(n * chunk, (n + 1) * chunk)
        o_ref[:, cols] = x_ref[:, cols] + jnp.dot(m_ref[:, :ATTN_WIDTH], wout_ref[:ATTN_WIDTH, cols],
                                                  preferred_element_type=F32)
        if first_stage:
            x_ref[:, cols] = xa_ref[:, cols]

    def out_g(n):
        cols = slice(n * chunk, (n + 1) * chunk)
        o_ref[:, cols] += jnp.dot(m_ref[:, ATTN_WIDTH:], wout_ref[ATTN_WIDTH:, cols],
                                  preferred_element_type=F32)

    n_q = ATTN_WIDTH // chunk
    n_g = GMLP_WIDTH // chunk
    n_out = o_ref.shape[1] // chunk
    units = [(b, hkv) for b in range(n_blk) for hkv in range(N_KV_HEADS)]
    assert len(units) == 2 * n_g

    def maybe_proj(c0):
        if first_stage:
            proj(c0)

    if first_stage:
        x_norm()
    for src, dst in zip(cast_in, cast_out):
        dst[...] = src[...].astype(BF16)
    if not second_stage:
        for c0 in range(0, z_ref.shape[1], chunk):
            proj(c0)
        x_ref[...] = xa_ref[...]
        return

    q_in = [heads_load(c * chunk) for c in range(n_q)]
    zk, k_sq = heads_load(col_k)
    zv = z_ref[:, col_v:col_v + KV_WIDTH]
    maybe_proj(0)
    q_ss = [heads_dot(sq) for _, sq in q_in]
    k_ss = heads_dot(k_sq)
    for c in range(n_q):
        q_finish(c, q_in[c][0], q_ss[c])
        if c % 2 == 1:
            maybe_proj((c + 1) // 2 * chunk)
    kv_finish(zk, k_ss, zv)
    maybe_proj((n_q - 1) * chunk)

    big = [col_k, col_v]
    for c in range(n_g):
        big += [col_gu + c * chunk, col_gv + c * chunk]
    scores, probs, act, scaled, spatial = {}, {}, {}, {}, {}
    for r in range(len(units) + 1):
        c = r // 2
        if r >= 1:
            weighted = att_values(*units[r - 1], probs[r - 1][0])
        if r < len(units):
            scores[r] = att_scores(*units[r])
        if r % 2 == 1:
            norm_ss = gmlp_norm_dot(act[c][2])
        elif c >= 1:
            spatial[c - 1] = gmlp_spatial_dots(c - 1, scaled[c - 1])
        if r >= 1:
            att_finish(*units[r - 1], weighted, probs[r - 1][1])
        if r < len(units):
            probs[r] = att_softmax(*units[r], scores[r])
        if r % 2 == 1:
            scaled[c] = gmlp_scale(c, act[c][1], norm_ss)
        else:
            if c >= 1:
                gmlp_finish(c - 1, act[c - 1][0], spatial[c - 1])
            if c < n_g:
                act[c] = gmlp_act(c)
        maybe_proj(big[r])
    carry_band()
    a_norm()
    maybe_proj(big[-1])
    for n in range(n_out):
        out_a(n)
    g_norm()
    for n in range(n_out):
        out_g(n)


def _mixer(x, pos, invf, gm, w_in, qg, kg, sinks, gvg, w_s, bst, ag, gg, w_out, *, seq_len, casts=()):
    t, d = x.shape
    tb = MIXER_TOKEN_TILE
    assert t % tb == 0 and seq_len % tb == 0 and tb % BLOCK == 0
    n_tiles = t // tb

    cast_specs = []
    for w in casts:
        rows, cols = w.shape
        nc = MIXER_CAST_COL_BLOCKS
        assert n_tiles % nc == 0
        nr = n_tiles // nc
        assert rows % (nr * BF16_SUBLANES) == 0 and cols % (nc * LANES) == 0
        cast_specs.append(pl.BlockSpec(
            (rows // nr, cols // nc),
            lambda i, nc=nc: (jnp.minimum(i, n_tiles - 1) // nc, jnp.minimum(i, n_tiles - 1) % nc)))

    def resident(shape):
        return pl.BlockSpec(shape, lambda i: (0,) * len(shape), pipeline_mode=pl.Buffered(1))

    def first_stage_tile(i):
        return jnp.minimum(i, n_tiles - 1)

    def second_stage_tile(i):
        return jnp.maximum(i - 1, 0)

    in_arrays = (x, pos, invf, gm, w_in, qg, kg, sinks, gvg, w_s, bst, ag, gg, w_out)
    assert len(in_arrays) == N_MIXER_INPUTS
    outs = pl.pallas_call(
        functools.partial(_mixer_kernel, tb=tb, tiles_per_seq=seq_len // tb, n_casts=len(casts)),
        out_shape=[jax.ShapeDtypeStruct((t, d), F32)] + [jax.ShapeDtypeStruct(w.shape, BF16) for w in casts],
        grid=(n_tiles + 1,),
        in_specs=[
            pl.BlockSpec((tb, d), lambda i: (first_stage_tile(i), 0)),
            resident(pos.shape),
            resident(invf.shape),
            resident(gm.shape),
            resident(w_in.shape),
            resident(qg.shape),
            resident(kg.shape),
            pl.BlockSpec(memory_space=pltpu.SMEM),
            resident(gvg.shape),
            resident(w_s.shape),
            resident(bst.shape),
            resident(ag.shape),
            resident(gg.shape),
            resident(w_out.shape),
        ] + cast_specs,
        out_specs=[pl.BlockSpec((tb, d), lambda i: (second_stage_tile(i), 0))] + cast_specs,
        scratch_shapes=[
            pltpu.VMEM((tb, d), BF16),
            pltpu.VMEM((tb, w_in.shape[1]), F32),
            pltpu.VMEM((tb, ATTN_WIDTH), BF16),
            pltpu.VMEM((2 * N_KV_HEADS, BLOCK + tb, LANES), BF16),
            pltpu.VMEM((2 * N_KV_HEADS, BLOCK + tb, LANES), BF16),
            pltpu.VMEM((tb, ATTN_WIDTH), F32),
            pltpu.VMEM((tb, GMLP_WIDTH), F32),
            pltpu.VMEM((tb, ATTN_WIDTH + GMLP_WIDTH), BF16),
            pltpu.VMEM((tb, d), F32),
        ],
        compiler_params=pltpu.CompilerParams(
            dimension_semantics=("arbitrary",),
            vmem_limit_bytes=V7X_VMEM_LIMIT_BYTES,
        ),
        name="mixer",
    )(*in_arrays, *casts)
    return outs[0], list(outs[1:])


def kernel(x, positions, ffn1_norm, ffn1_w_gate, ffn1_w_up, ffn1_w_down, mix_norm, w_in,
           q_norm, k_norm, attn_sinks, gmlp_v_norm, gmlp_w_s, gmlp_b_s, attn_out_norm,
           gmlp_out_norm, w_out, ffn2_norm, ffn2_w_gate, ffn2_w_up, ffn2_w_down):
    b, s, d = x.shape
    depth = w_in.shape[0]
    xt = x.reshape(b * s, d)
    pos = positions.reshape(b * s // MIXER_TOKEN_TILE, 1, MIXER_TOKEN_TILE)
    invf = (ROPE_THETA ** (-jnp.arange(0, ROT_DIM, 2, dtype=F32) / ROT_DIM)).reshape(ROT_DIM // 2, 1)

    for l in range(depth):
        xt = _ffn(xt, ffn1_norm[l].reshape(1, d), ffn1_w_gate[l].astype(BF16),
                  ffn1_w_up[l].astype(BF16), ffn1_w_down[l].astype(BF16))
        xt, (g2_b, u2_b, d2_b) = _mixer(
            xt, pos, invf, mix_norm[l].reshape(1, d), w_in[l].astype(BF16),
            (jnp.tile(q_norm[l], 2) * (HEAD_DIM ** -0.5)).reshape(1, LANES),
            jnp.tile(k_norm[l], 2).reshape(1, LANES),
            attn_sinks[l], gmlp_v_norm[l].reshape(1, GMLP_WIDTH), gmlp_w_s[l], gmlp_b_s[l].T,
            attn_out_norm[l].reshape(1, ATTN_WIDTH), gmlp_out_norm[l].reshape(1, GMLP_WIDTH),
            w_out[l].astype(BF16), seq_len=s, casts=[ffn2_w_gate[l], ffn2_w_up[l], ffn2_w_down[l]])
        xt = _ffn(xt, ffn2_norm[l].reshape(1, d), g2_b, u2_b, d2_b)
    return xt.reshape(b, s, d)
```

```python
import functools

import jax
import jax.numpy as jnp
from jax import lax
from jax.experimental import pallas as pl
from jax.experimental.pallas import tpu as pltpu

HEAD_DIM = 64
N_Q_HEADS = 16
N_KV_HEADS = 4
ATTN_WIDTH = N_Q_HEADS * HEAD_DIM
KV_WIDTH = N_KV_HEADS * HEAD_DIM
BLOCK = 128
ROT_DIM = HEAD_DIM // 4
ROPE_THETA = 500000.0
GMLP_GROUP_DIM = 128
GMLP_WIDTH = 1024
N_GMLP_GROUPS = GMLP_WIDTH // GMLP_GROUP_DIM
EPS = 1e-6
NEG_INF = -1e30

LANES = 128
BF16_SUBLANES = 16
V7X_VMEM_LIMIT_BYTES = 60000 * 1024

FFN_TOKEN_TILE = 1024
FFN_FF_TILE = 512
FFN_NORM_CHUNKS = 4
MIXER_TOKEN_TILE = 256
MIXER_CAST_COL_BLOCKS = 4
N_MIXER_INPUTS = 14

F32 = jnp.float32
BF16 = jnp.bfloat16


def _rmsnorm_rows(x, gain):
    y = x * lax.rsqrt(jnp.mean(x * x, axis=-1, keepdims=True) + EPS)
    return y * gain


def _ffn_kernel(x_ref, gain_ref, wg_ref, wu_ref, wd_ref, o_ref, h_ref):
    f = pl.program_id(1)

    def activate(g, u):
        return (jax.nn.silu(g) * (0.5 * u)).astype(BF16)

    @pl.when(f == 0)
    def _():
        rows = h_ref.shape[0] // FFN_NORM_CHUNKS
        hs, gus = [], []

        def norm(r):
            sl = slice(r * rows, (r + 1) * rows)
            h = _rmsnorm_rows(x_ref[sl, :], gain_ref[...]).astype(BF16)
            h_ref[sl, :] = h
            return h

        def gate_up(h):
            return (jnp.dot(h, wg_ref[...], preferred_element_type=F32),
                    jnp.dot(h, wu_ref[...], preferred_element_type=F32))

        def down(r, a):
            sl = slice(r * rows, (r + 1) * rows)
            o_ref[sl, :] = x_ref[sl, :] + jnp.dot(a, wd_ref[...], preferred_element_type=F32)

        hs.append(norm(0))
        gus.append(gate_up(hs[0]))
        for r in range(FFN_NORM_CHUNKS):
            if r + 1 < FFN_NORM_CHUNKS:
                hs.append(norm(r + 1))
                gus.append(gate_up(hs[r + 1]))
            down(r, activate(*gus[r]))

    @pl.when(f > 0)
    def _():
        h = h_ref[...]
        g = jnp.dot(h, wg_ref[...], preferred_element_type=F32)
        u = jnp.dot(h, wu_ref[...], preferred_element_type=F32)
        o_ref[...] += jnp.dot(activate(g, u), wd_ref[...], preferred_element_type=F32)


def _ffn(x, gain, w_gate, w_up, w_down):
    t, d = x.shape
    d_ff = w_gate.shape[1]
    tm, tf = FFN_TOKEN_TILE, FFN_FF_TILE
    assert t % tm == 0 and d_ff % tf == 0
    return pl.pallas_call(
        _ffn_kernel,
        out_shape=jax.ShapeDtypeStruct((t, d), F32),
        grid=(t // tm, d_ff // tf),
        in_specs=[
            pl.BlockSpec((tm, d), lambda i, f: (i, 0)),
            pl.BlockSpec((1, d), lambda i, f: (0, 0)),
            pl.BlockSpec((d, tf), lambda i, f: (0, f)),
            pl.BlockSpec((d, tf), lambda i, f: (0, f)),
            pl.BlockSpec((tf, d), lambda i, f: (f, 0)),
        ],
        out_specs=pl.BlockSpec((tm, d), lambda i, f: (i, 0)),
        scratch_shapes=[pltpu.VMEM((tm, d), BF16)],
        compiler_params=pltpu.CompilerParams(
            dimension_semantics=("arbitrary", "arbitrary"),
            vmem_limit_bytes=V7X_VMEM_LIMIT_BYTES,
        ),
        name="ffn",
    )(x, gain, w_gate, w_up, w_down)


def _block_diag_ones(size, group):
    r = lax.broadcasted_iota(jnp.int32, (size, size), 0) // group
    c = lax.broadcasted_iota(jnp.int32, (size, size), 1) // group
    return jnp.where(r == c, 1.0, 0.0).astype(BF16)


def _mixer_kernel(*refs, tb, tiles_per_seq, n_casts):
    (xa_ref, pos_ref, invf_ref, gm_ref, win_ref, qg_ref, kg_ref, sinks_ref,
     gvg_ref, ws_ref, bst_ref, ag_ref, gg_ref, wout_ref) = refs[:N_MIXER_INPUTS]
    cast_in = refs[N_MIXER_INPUTS:N_MIXER_INPUTS + n_casts]
    o_ref = refs[N_MIXER_INPUTS + n_casts]
    cast_out = refs[N_MIXER_INPUTS + n_casts + 1:N_MIXER_INPUTS + 2 * n_casts + 1]
    h_ref, z_ref, q_ref, kpad_ref, vpad_ref, a_ref, g_ref, m_ref, x_ref = refs[N_MIXER_INPUTS + 2 * n_casts + 1:]
    i = pl.program_id(0)
    n_blk = tb // BLOCK
    chunk = 2 * LANES
    half = ROT_DIM // 2
    col_k = ATTN_WIDTH
    col_v = col_k + KV_WIDTH
    col_gu = col_v + KV_WIDTH
    col_gv = col_gu + GMLP_WIDTH

    @pl.when(i == 0)
    def _():
        z_ref[...] = jnp.zeros_like(z_ref)
        x_ref[...] = jnp.zeros_like(x_ref)
        kpad_ref[:, :BLOCK, :] = jnp.zeros((2 * N_KV_HEADS, BLOCK, LANES), BF16)
        vpad_ref[:, :BLOCK, :] = jnp.zeros((2 * N_KV_HEADS, BLOCK, LANES), BF16)

    lane = lax.broadcasted_iota(jnp.int32, (1, LANES), 1)
    lo_half = lane < HEAD_DIM

    ang = invf_ref[...] * pos_ref[jnp.maximum(i - 1, 0)].astype(F32)
    cos, sin = jnp.cos(ang), jnp.sin(ang)
    rest_one = jnp.ones((HEAD_DIM - ROT_DIM, tb), F32)
    rest_zero = jnp.zeros((HEAD_DIM - ROT_DIM, tb), F32)
    zero = jnp.zeros((half, tb), F32)
    rope_c = jnp.concatenate([cos, cos, rest_one] * 2, axis=0).T
    rope_s1 = jnp.concatenate([-sin, zero, rest_zero] * 2, axis=0).T
    rope_s2 = jnp.concatenate([zero, sin, rest_zero] * 2, axis=0).T

    bd_head = _block_diag_ones(chunk, HEAD_DIM)
    bd_group = _block_diag_ones(chunk, GMLP_GROUP_DIM)

    def rope(t):
        return t * rope_c + pltpu.roll(t, LANES - half, 1) * rope_s1 + pltpu.roll(t, half, 1) * rope_s2

    def x_norm():
        h_ref[...] = _rmsnorm_rows(xa_ref[...], gm_ref[...]).astype(BF16)

    def proj(c0):
        cols = slice(c0, c0 + chunk)
        z_ref[:, cols] = jnp.dot(h_ref[...], win_ref[:, cols], preferred_element_type=F32)

    def heads_load(c0):
        z = z_ref[:, c0:c0 + chunk]
        return z, (z * z).astype(BF16)

    def heads_dot(sq):
        return jnp.dot(sq, bd_head, preferred_element_type=F32)

    def heads_finish(z, ss, gain):
        zn = z * lax.rsqrt(ss * (1.0 / HEAD_DIM) + EPS)
        return [rope(zn[:, j * LANES:(j + 1) * LANES] * gain) for j in range(2)]

    def q_finish(c, z, ss):
        s0, s1 = heads_finish(z, ss, qg_ref[...])
        q_ref[:, c * chunk:(c + 1) * chunk] = jnp.concatenate([s0, s1], axis=1).astype(BF16)

    def place_heads(slab, dst_ref, j):
        rolled = pltpu.roll(slab, HEAD_DIM, 1)
        zero_slab = jnp.zeros_like(slab)
        cur = slice(BLOCK, BLOCK + tb)
        dst_ref[4 * j + 0, cur, :] = jnp.where(lo_half, slab, zero_slab).astype(BF16)
        dst_ref[4 * j + 1, cur, :] = jnp.where(lo_half, zero_slab, rolled).astype(BF16)
        dst_ref[4 * j + 2, cur, :] = jnp.where(lo_half, rolled, zero_slab).astype(BF16)
        dst_ref[4 * j + 3, cur, :] = jnp.where(lo_half, zero_slab, slab).astype(BF16)

    def kv_finish(zk, ss, zv):
        k_slabs = heads_finish(zk, ss, kg_ref[...])
        for j in range(2):
            place_heads(k_slabs[j], kpad_ref, j)
            place_heads(zv[:, j * LANES:(j + 1) * LANES], vpad_ref, j)

    srow = lax.broadcasted_iota(jnp.int32, (2 * BLOCK, 4 * BLOCK), 0) % BLOCK
    scol = lax.broadcasted_iota(jnp.int32, (2 * BLOCK, 4 * BLOCK), 1) % (2 * BLOCK)
    row_first_slab = lax.broadcasted_iota(jnp.int32, (2 * BLOCK, 1), 0) < BLOCK
    seq_start = ((i + tiles_per_seq - 1) % tiles_per_seq) == 0
    ones_lo = jnp.broadcast_to(jnp.where(lo_half, 1.0, 0.0).astype(BF16), (2 * BLOCK, LANES))
    ones_hi = jnp.broadcast_to(jnp.where(lo_half, 0.0, 1.0).astype(BF16), (2 * BLOCK, LANES))

    def band_mask(b):
        prev_off = jnp.where(seq_start, BLOCK, 0) if b == 0 else 0
        return (((scol < BLOCK) & (scol > srow + prev_off))
                | ((scol >= BLOCK) & (scol - BLOCK <= srow)))

    def att_scores(b, hkv):
        qrows = slice(b * BLOCK, (b + 1) * BLOCK)
        krows = slice(b * BLOCK, (b + 2) * BLOCK)
        q2 = jnp.concatenate([q_ref[qrows, (2 * hkv) * LANES:(2 * hkv + 1) * LANES],
                              q_ref[qrows, (2 * hkv + 1) * LANES:(2 * hkv + 2) * LANES]], axis=0)
        kcat = jnp.concatenate([kpad_ref[2 * hkv, krows, :], kpad_ref[2 * hkv + 1, krows, :]], axis=0)
        return lax.dot_general(q2, kcat, (((1,), (1,)), ((), ())), preferred_element_type=F32)

    def att_softmax(b, hkv, s):
        s = jnp.where(band_mask(b), s, NEG_INF)
        sink_lo = jnp.where(row_first_slab, sinks_ref[4 * hkv], sinks_ref[4 * hkv + 2])
        sink_hi = jnp.where(row_first_slab, sinks_ref[4 * hkv + 1], sinks_ref[4 * hkv + 3])
        s_lo, s_hi = s[:, :2 * BLOCK], s[:, 2 * BLOCK:]
        m_lo = jnp.maximum(jnp.max(s_lo, axis=1, keepdims=True), sink_lo)
        m_hi = jnp.maximum(jnp.max(s_hi, axis=1, keepdims=True), sink_hi)
        p = jnp.concatenate([jnp.exp(s_lo - m_lo), jnp.exp(s_hi - m_hi)], axis=1).astype(BF16)
        sink_mass = jnp.where(lo_half, jnp.exp(sink_lo - m_lo), jnp.exp(sink_hi - m_hi))
        return p, sink_mass

    def att_values(b, hkv, p):
        krows = slice(b * BLOCK, (b + 2) * BLOCK)
        vcat = jnp.concatenate(
            [jnp.concatenate([vpad_ref[2 * hkv, krows, :], ones_lo], axis=1),
             jnp.concatenate([vpad_ref[2 * hkv + 1, krows, :], ones_hi], axis=1)], axis=0)
        return jnp.dot(p, vcat, preferred_element_type=F32)

    def att_finish(b, hkv, o2, sink_mass):
        qrows = slice(b * BLOCK, (b + 1) * BLOCK)
        out = o2[:, :LANES] / (o2[:, LANES:] + sink_mass)
        a_ref[qrows, (2 * hkv) * LANES:(2 * hkv + 1) * LANES] = out[:BLOCK]
        a_ref[qrows, (2 * hkv + 1) * LANES:(2 * hkv + 2) * LANES] = out[BLOCK:]

    def carry_band():
        last = slice(tb, tb + BLOCK)
        kpad_ref[:, :BLOCK, :] = kpad_ref[:, last, :]
        vpad_ref[:, :BLOCK, :] = vpad_ref[:, last, :]

    trow = lax.broadcasted_iota(jnp.int32, (BLOCK, BLOCK), 0)
    tcol = lax.broadcasted_iota(jnp.int32, (BLOCK, BLOCK), 1)
    causal = tcol <= trow

    def gmlp_act(c):
        gu = jax.nn.gelu(z_ref[:, col_gu + c * chunk:col_gu + (c + 1) * chunk])
        gv = jax.nn.gelu(z_ref[:, col_gv + c * chunk:col_gv + (c + 1) * chunk])
        return gu, gv, (gv * gv).astype(BF16)

    def gmlp_norm_dot(sq):
        return jnp.dot(sq, bd_group, preferred_element_type=F32)

    def gmlp_scale(c, gv, ss):
        gv = gv * lax.rsqrt(ss * (1.0 / GMLP_GROUP_DIM) + EPS)
        return (gv * gvg_ref[:, c * chunk:(c + 1) * chunk]).astype(BF16)

    def gmlp_spatial_dots(c, gvn):
        sps = []
        for jj in range(2):
            w = jnp.where(causal, ws_ref[2 * c + jj], 0.0).astype(BF16)
            for b in range(n_blk):
                rws = slice(b * BLOCK, (b + 1) * BLOCK)
                sps.append(jnp.dot(w, gvn[rws, jj * LANES:(jj + 1) * LANES], preferred_element_type=F32))
        return sps

    def gmlp_finish(c, gu, sps):
        for jj in range(2):
            grp = 2 * c + jj
            bias = bst_ref[:, grp:grp + 1]
            for b in range(n_blk):
                rws = slice(b * BLOCK, (b + 1) * BLOCK)
                g_ref[rws, grp * LANES:(grp + 1) * LANES] = (
                    gu[rws, jj * LANES:(jj + 1) * LANES] * (sps[jj * n_blk + b] + bias))

    def a_norm():
        m_ref[:, :ATTN_WIDTH] = _rmsnorm_rows(a_ref[...], ag_ref[...]).astype(BF16)

    def g_norm():
        m_ref[:, ATTN_WIDTH:] = _rmsnorm_rows(g_ref[...], gg_ref[...]).astype(BF16)

    def out_proj(n):
        cols = slice(n * chunk, (n + 1) * chunk)
        o_ref[:, cols] = x_ref[:, cols] + jnp.dot(m_ref[...], wout_ref[:, cols], preferred_element_type=F32)
        x_ref[:, cols] = xa_ref[:, cols]

    n_q = ATTN_WIDTH // chunk
    n_g = GMLP_WIDTH // chunk
    n_out = o_ref.shape[1] // chunk
    units = [(b, hkv) for b in range(n_blk) for hkv in range(N_KV_HEADS)]
    assert len(units) == 2 * n_g

    x_norm()
    for src, dst in zip(cast_in, cast_out):
        dst[...] = src[...].astype(BF16)
    q_in = [heads_load(c * chunk) for c in range(n_q)]
    zk, k_sq = heads_load(col_k)
    zv = z_ref[:, col_v:col_v + KV_WIDTH]
    proj(0)
    q_ss = [heads_dot(sq) for _, sq in q_in]
    k_ss = heads_dot(k_sq)
    for c in range(n_q):
        q_finish(c, q_in[c][0], q_ss[c])
        if c % 2 == 1:
            proj((c + 1) // 2 * chunk)
    kv_finish(zk, k_ss, zv)
    proj((n_q - 1) * chunk)

    big = [col_k, col_v]
    for c in range(n_g):
        big += [col_gu + c * chunk, col_gv + c * chunk]
    scores, probs, act, scaled, spatial = {}, {}, {}, {}, {}
    for r in range(len(units) + 1):
        c = r // 2
        if r >= 1:
            weighted = att_values(*units[r - 1], probs[r - 1][0])
        if r < len(units):
            scores[r] = att_scores(*units[r])
        if r % 2 == 1:
            norm_ss = gmlp_norm_dot(act[c][2])
        elif c >= 1:
            spatial[c - 1] = gmlp_spatial_dots(c - 1, scaled[c - 1])
        if r >= 1:
            att_finish(*units[r - 1], weighted, probs[r - 1][1])
        if r < len(units):
            probs[r] = att_softmax(*units[r], scores[r])
        if r % 2 == 1:
            scaled[c] = gmlp_scale(c, act[c][1], norm_ss)
        else:
            if c >= 1:
                gmlp_finish(c - 1, act[c - 1][0], spatial[c - 1])
            if c < n_g:
                act[c] = gmlp_act(c)
        proj(big[r])
    carry_band()
    a_norm()
    g_norm()
    proj(big[-1])
    for n in range(n_out):
        out_proj(n)


def _mixer(x, pos, invf, gm, w_in, qg, kg, sinks, gvg, w_s, bst, ag, gg, w_out, *, seq_len, casts=()):
    t, d = x.shape
    tb = MIXER_TOKEN_TILE
    assert t % tb == 0 and seq_len % tb == 0 and tb % BLOCK == 0
    n_tiles = t // tb

    cast_specs = []
    for w in casts:
        rows, cols = w.shape
        nc = MIXER_CAST_COL_BLOCKS
        assert n_tiles % nc == 0
        nr = n_tiles // nc
        assert rows % (nr * BF16_SUBLANES) == 0 and cols % (nc * LANES) == 0
        cast_specs.append(pl.BlockSpec(
            (rows // nr, cols // nc),
            lambda i, nc=nc: (jnp.minimum(i, n_tiles - 1) // nc, jnp.minimum(i, n_tiles - 1) % nc)))

    def resident(shape):
        return pl.BlockSpec(shape, lambda i: (0,) * len(shape), pipeline_mode=pl.Buffered(1))

    def first_stage_tile(i):
        return jnp.minimum(i, n_tiles - 1)

    def second_stage_tile(i):
        return jnp.maximum(i - 1, 0)

    in_arrays = (x, pos, invf, gm, w_in, qg, kg, sinks, gvg, w_s, bst, ag, gg, w_out)
    assert len(in_arrays) == N_MIXER_INPUTS
    outs = pl.pallas_call(
        functools.partial(_mixer_kernel, tb=tb, tiles_per_seq=seq_len // tb, n_casts=len(casts)),
        out_shape=[jax.ShapeDtypeStruct((t, d), F32)] + [jax.ShapeDtypeStruct(w.shape, BF16) for w in casts],
        grid=(n_tiles + 1,),
        in_specs=[
            pl.BlockSpec((tb, d), lambda i: (first_stage_tile(i), 0)),
            resident(pos.shape),
            resident(invf.shape),
            resident(gm.shape),
            resident(w_in.shape),
            resident(qg.shape),
            resident(kg.shape),
            pl.BlockSpec(memory_space=pltpu.SMEM),
            resident(gvg.shape),
            resident(w_s.shape),
            resident(bst.shape),
            resident(ag.shape),
            resident(gg.shape),
            resident(w_out.shape),
        ] + cast_specs,
        out_specs=[pl.BlockSpec((tb, d), lambda i: (second_stage_tile(i), 0))] + cast_specs,
        scratch_shapes=[
            pltpu.VMEM((tb, d), BF16),
            pltpu.VMEM((tb, w_in.shape[1]), F32),
            pltpu.VMEM((tb, ATTN_WIDTH), BF16),
            pltpu.VMEM((2 * N_KV_HEADS, BLOCK + tb, LANES), BF16),
            pltpu.VMEM((2 * N_KV_HEADS, BLOCK + tb, LANES), BF16),
            pltpu.VMEM((tb, ATTN_WIDTH), F32),
            pltpu.VMEM((tb, GMLP_WIDTH), F32),
            pltpu.VMEM((tb, ATTN_WIDTH + GMLP_WIDTH), BF16),
            pltpu.VMEM((tb, d), F32),
        ],
        compiler_params=pltpu.CompilerParams(
            dimension_semantics=("arbitrary",),
            vmem_limit_bytes=V7X_VMEM_LIMIT_BYTES,
        ),
        name="mixer",
    )(*in_arrays, *casts)
    return outs[0], list(outs[1:])


def kernel(x, positions, ffn1_norm, ffn1_w_gate, ffn1_w_up, ffn1_w_down, mix_norm, w_in,
           q_norm, k_norm, attn_sinks, gmlp_v_norm, gmlp_w_s, gmlp_b_s, attn_out_norm,
           gmlp_out_norm, w_out, ffn2_norm, ffn2_w_gate, ffn2_w_up, ffn2_w_down):
    b, s, d = x.shape
    depth = w_in.shape[0]
    xt = x.reshape(b * s, d)
    pos = positions.reshape(b * s // MIXER_TOKEN_TILE, 1, MIXER_TOKEN_TILE)
    invf = (ROPE_THETA ** (-jnp.arange(0, ROT_DIM, 2, dtype=F32) / ROT_DIM)).reshape(ROT_DIM // 2, 1)

    for l in range(depth):
        xt = _ffn(xt, ffn1_norm[l].reshape(1, d), ffn1_w_gate[l].astype(BF16),
                  ffn1_w_up[l].astype(BF16), ffn1_w_down[l].astype(BF16))
        xt, (g2_b, u2_b, d2_b) = _mixer(
            xt, pos, invf, mix_norm[l].reshape(1, d), w_in[l].astype(BF16),
            (jnp.tile(q_norm[l], 2) * (HEAD_DIM ** -0.5)).reshape(1, LANES),
            jnp.tile(k_norm[l], 2).reshape(1, LANES),
            attn_sinks[l], gmlp_v_norm[l].reshape(1, GMLP_WIDTH), gmlp_w_s[l], gmlp_b_s[l].T,
            attn_out_norm[l].reshape(1, ATTN_WIDTH), gmlp_out_norm[l].reshape(1, GMLP_WIDTH),
            w_out[l].astype(BF16), seq_len=s, casts=[ffn2_w_gate[l], ffn2_w_up[l], ffn2_w_down[l]])
        xt = _ffn(xt, ffn2_norm[l].reshape(1, d), g2_b, u2_b, d2_b)
    return xt.reshape(b, s, d)
```
